```python
import math
import jax, jax.numpy as jnp
from jax import lax
import numpy as np

D_MODEL = 2048
BATCH = 4
SEQ = 2048
DEPTH = 1

MLA_HEADS = 8
QK_NOPE = 128
QK_ROPE = 64
V_HEAD = 128
Q_LORA = 512
KV_LORA = 512
ROPE_THETA = 10000.0
SWA_Q_HEADS = 16
SWA_KV_HEADS = 2
SWA_GROUP = SWA_Q_HEADS // SWA_KV_HEADS
SWA_HEAD_DIM = 64
WINDOW = 128
BLOCK = 128
REL_BUCKETS = 32
REL_MAX_DIST = 128
MLP_HIDDEN = 4 * D_MODEL
EPS = 1e-6
MLA_WIDTH = MLA_HEADS * V_HEAD
SWA_WIDTH = SWA_Q_HEADS * SWA_HEAD_DIM
MIX_WIDTH = MLA_WIDTH + SWA_WIDTH
SWA_KV_WIDTH = SWA_KV_HEADS * SWA_HEAD_DIM
IN_WIDTH = Q_LORA + KV_LORA + QK_ROPE + SWA_WIDTH + 2 * SWA_KV_WIDTH
IN_SPLIT_POINTS = (
    Q_LORA,
    Q_LORA + KV_LORA,
    Q_LORA + KV_LORA + QK_ROPE,
    Q_LORA + KV_LORA + QK_ROPE + SWA_WIDTH,
    Q_LORA + KV_LORA + QK_ROPE + SWA_WIDTH + SWA_KV_WIDTH,
)

kernel_name = "hymba_mla_swa_sink_t5_relu2"


def rms_norm(x, g):
    xf = x.astype(jnp.float32)
    xf = xf * lax.rsqrt(jnp.mean(xf * xf, axis=-1, keepdims=True) + EPS)
    return (xf * g.astype(jnp.float32)).astype(x.dtype)


def rope_angles(positions):
    inv = 1.0 / (ROPE_THETA ** (jnp.arange(0, QK_ROPE, 2, dtype=jnp.float32) / QK_ROPE))
    ang = positions.astype(jnp.float32)[..., None] * inv
    return jnp.cos(ang), jnp.sin(ang)


def apply_rope(x, cos, sin):
    xf = x.astype(jnp.float32)
    x1, x2 = xf[..., : QK_ROPE // 2], xf[..., QK_ROPE // 2:]
    out = jnp.concatenate([x1 * cos - x2 * sin, x2 * cos + x1 * sin], axis=-1)
    return out.astype(x.dtype)


def t5_bucket(dist):
    n = jnp.maximum(dist, 0)
    max_exact = REL_BUCKETS // 2
    n_safe = jnp.maximum(n, 1).astype(jnp.float32)
    large = max_exact + (jnp.log(n_safe / max_exact) / math.log(REL_MAX_DIST / max_exact)
                         * (REL_BUCKETS - max_exact)).astype(jnp.int32)
    large = jnp.minimum(large, REL_BUCKETS - 1)
    return jnp.where(n < max_exact, n, large)


def mla_attention(c_q, c_kv, k_rope_raw, cos, sin, q_a_norm, w_q_b, kv_a_norm, w_kv_b):
    B, S, _ = c_q.shape
    nb = S // BLOCK
    q = (rms_norm(c_q, q_a_norm) @ w_q_b).reshape(B, S, MLA_HEADS, QK_NOPE + QK_ROPE)
    q_nope = q[..., :QK_NOPE]
    q_rope = apply_rope(q[..., QK_NOPE:], cos[:, :, None, :], sin[:, :, None, :])
    k_rope = apply_rope(k_rope_raw, cos, sin)
    kv = (rms_norm(c_kv, kv_a_norm) @ w_kv_b).reshape(B, S, MLA_HEADS, QK_NOPE + V_HEAD)
    k_nope, v = kv[..., :QK_NOPE], kv[..., QK_NOPE:]
    qn_blocks = q_nope.reshape(B, nb, BLOCK, MLA_HEADS, QK_NOPE).transpose(1, 0, 2, 3, 4)
    qr_blocks = q_rope.reshape(B, nb, BLOCK, MLA_HEADS, QK_ROPE).transpose(1, 0, 2, 3, 4)
    key_idx = jnp.arange(S)
    scale = (QK_NOPE + QK_ROPE) ** -0.5

    def block_fn(args):
        qn, qr, blk = args
        s = (jnp.einsum('bqhd,bkhd->bhqk', qn, k_nope)
             + jnp.einsum('bqhr,bkr->bhqk', qr, k_rope)).astype(jnp.float32) * scale
        q_idx = blk * BLOCK + jnp.arange(BLOCK)
        causal = key_idx[None, :] <= q_idx[:, None]
        s = jnp.where(causal, s, -jnp.inf)
        p = jax.nn.softmax(s, axis=-1).astype(v.dtype)
        return jnp.einsum('bhqk,bkhd->bqhd', p, v)

    out = lax.map(block_fn, (qn_blocks, qr_blocks, jnp.arange(nb)))
    return out.transpose(1, 0, 2, 3, 4).reshape(B, S, MLA_WIDTH)


def _band(t):
    prev = jnp.concatenate([jnp.zeros_like(t[:, :1]), t[:, :-1]], axis=1)
    return jnp.concatenate([prev, t], axis=2)


def swa_sink_attention(q, k, v, positions, rel_bias, sinks):
    B, S, _ = q.shape
    nb = S // BLOCK
    q = q.reshape(B, nb, BLOCK, SWA_KV_HEADS, SWA_GROUP, SWA_HEAD_DIM)
    k_band = _band(k.reshape(B, nb, BLOCK, SWA_KV_HEADS, SWA_HEAD_DIM))
    v_band = _band(v.reshape(B, nb, BLOCK, SWA_KV_HEADS, SWA_HEAD_DIM))
    pos = positions.reshape(B, nb, BLOCK)
    pos_band = _band(pos)
    s = jnp.einsum('bnqhgd,bnkhd->bnhgqk', q, k_band).astype(jnp.float32) * (SWA_HEAD_DIM ** -0.5)
    bucket = t5_bucket(pos[:, :, :, None] - pos_band[:, :, None, :])
    bias = rel_bias.astype(jnp.float32)[bucket]
    bias = bias.reshape(B, nb, BLOCK, 2 * BLOCK, SWA_KV_HEADS, SWA_GROUP).transpose(0, 1, 4, 5, 2, 3)
    s = s + bias
    qi = jnp.arange(BLOCK)[:, None] + BLOCK
    kj = jnp.arange(2 * BLOCK)[None, :]
    d = qi - kj
    in_window = (d >= 0) & (d < WINDOW)
    key_global = jnp.arange(nb)[:, None, None] * BLOCK + kj[None] - BLOCK
    valid = in_window[None] & (key_global >= 0)
    s = jnp.where(valid[None, :, None, None], s, -jnp.inf)
    sink = sinks.astype(jnp.float32).reshape(SWA_KV_HEADS, SWA_GROUP)[None, None, :, :, None, None]
    m = jnp.maximum(jnp.max(s, axis=-1, keepdims=True), sink)
    e = jnp.exp(s - m)
    p = e / (jnp.sum(e, axis=-1, keepdims=True) + jnp.exp(sink - m))
    out = jnp.einsum('bnhgqk,bnkhd->bnqhgd', p.astype(v_band.dtype), v_band)
    return out.reshape(B, S, SWA_WIDTH)


def setup_inputs(seed: int = 0) -> dict:
    key = jax.random.key(seed)
    ks = jax.random.split(key, 16)
    f32 = jnp.float32

    def w(k, shape, fan_in):
        return jax.random.normal(k, shape, f32) * (fan_in ** -0.5)

    def gain(k, shape):
        return 1.0 + 0.05 * jax.random.normal(k, shape, f32)

    x = jax.random.normal(ks[0], (BATCH, SEQ, D_MODEL), f32)
    offset = jax.random.randint(ks[1], (BATCH, 1), 0, 1024, dtype=jnp.int32)
    positions = offset + jnp.arange(SEQ, dtype=jnp.int32)[None, :]
    return {
        "x": x,
        "positions": positions,
        "rel_bias": 0.5 * jax.random.normal(ks[2], (REL_BUCKETS, SWA_Q_HEADS), f32),
        "attn_norm": gain(ks[3], (DEPTH, D_MODEL)),
        "w_in": w(ks[4], (DEPTH, D_MODEL, IN_WIDTH), D_MODEL),
        "q_a_norm": gain(ks[5], (DEPTH, Q_LORA)),
        "w_q_b": w(ks[6], (DEPTH, Q_LORA, MLA_HEADS * (QK_NOPE + QK_ROPE)), Q_LORA),
        "kv_a_norm": gain(ks[7], (DEPTH, KV_LORA)),
        "w_kv_b": w(ks[8], (DEPTH, KV_LORA, MLA_HEADS * (QK_NOPE + V_HEAD)), KV_LORA),
        "sinks": 0.5 * jax.random.normal(ks[9], (DEPTH, SWA_Q_HEADS), f32),
        "w_out": w(ks[10], (DEPTH, MIX_WIDTH, D_MODEL), MIX_WIDTH),
        "mlp_norm": gain(ks[11], (DEPTH, D_MODEL)),
        "w_up": w(ks[12], (DEPTH, D_MODEL, MLP_HIDDEN), D_MODEL),
        "w_down": w(ks[13], (DEPTH, MLP_HIDDEN, D_MODEL), MLP_HIDDEN),
        "final_norm": gain(ks[14], (D_MODEL,)),
    }


def reference(x, positions, rel_bias, attn_norm, w_in, q_a_norm, w_q_b, kv_a_norm, w_kv_b,
              sinks, w_out, mlp_norm, w_up, w_down, final_norm):
    h = x
    cos, sin = rope_angles(positions)
    for l in range(DEPTH):
        a = rms_norm(h, attn_norm[l])
        proj = a @ w_in[l]
        c_q, c_kv, k_rope, q_s, k_s, v_s = jnp.split(proj, IN_SPLIT_POINTS, axis=-1)
        y_mla = mla_attention(c_q, c_kv, k_rope, cos, sin,
                              q_a_norm[l], w_q_b[l], kv_a_norm[l], w_kv_b[l])
        y_swa = swa_sink_attention(q_s, k_s, v_s, positions, rel_bias, sinks[l])
        h = h + jnp.concatenate([y_mla, y_swa], axis=-1) @ w_out[l]
        m = rms_norm(h, mlp_norm[l])
        h = h + jnp.square(jax.nn.relu(m @ w_up[l])) @ w_down[l]
    return rms_norm(h, final_norm)
```

```python
import jax
import jax.numpy as jnp
import numpy as np
from jax import lax
from jax.experimental import pallas as pl
from jax.experimental.pallas import tpu as pltpu

D_MODEL = 2048
MLA_HEADS = 8
QK_NOPE = 128
QK_ROPE = 64
V_HEAD = 128
Q_LORA = 512
KV_LORA = 512
ROPE_THETA = 10000.0
SWA_Q_HEADS = 16
SWA_KV_HEADS = 2
SWA_GROUP = SWA_Q_HEADS // SWA_KV_HEADS
SWA_HEAD_DIM = 64
WINDOW = 128
BLOCK = 128
REL_BUCKETS = 32
REL_MAX_DIST = 128
MLP_HIDDEN = 4 * D_MODEL
EPS = 1e-6
MLA_WIDTH = MLA_HEADS * V_HEAD
SWA_WIDTH = SWA_Q_HEADS * SWA_HEAD_DIM
SWA_KV_WIDTH = SWA_KV_HEADS * SWA_HEAD_DIM

LANES = 128
QK_PAD = 2 * LANES
MLA_SCALE = (QK_NOPE + QK_ROPE) ** -0.5
SWA_SCALE = SWA_HEAD_DIM ** -0.5

_C_Q = 0
_C_KV = _C_Q + Q_LORA
_C_QS = _C_KV + KV_LORA
_C_KS = _C_QS + SWA_WIDTH
_C_VS = _C_KS + SWA_KV_WIDTH
_C_KR = _C_VS + SWA_KV_WIDTH
IN_PAD = _C_KR + LANES

VMEM_LIMIT = 56 * 1024 * 1024

PROJ_TM = 512
MLA_TQ = 256
OUT_TM = 512
MLP_TM = 1024
MLP_TH = 512

bf16 = jnp.bfloat16
f32 = jnp.float32


def _rms(x, g):
    return x * lax.rsqrt(jnp.mean(x * x, axis=-1, keepdims=True) + EPS) * g


def _dot(a, b):
    return jnp.dot(a, b, preferred_element_type=f32)


def _dot_nt(a, b):
    return lax.dot_general(a, b, (((1,), (1,)), ((), ())), preferred_element_type=f32)


def _rope128(t, cos, sin):
    lane = lax.broadcasted_iota(jnp.int32, t.shape, 1)
    fwd = pltpu.roll(t, QK_ROPE // 2, axis=1)
    bwd = pltpu.roll(t, LANES - QK_ROPE // 2, axis=1)
    rot = jnp.where((lane & (QK_ROPE // 2)) == 0, -bwd, fwd)
    return t * cos + rot * sin


def _proj_kernel(x_ref, pos_ref, inv_ref, gin_ref, win_ref, gq_ref, wq_ref, gkv_ref, wkv_ref,
                 q_ref, k_ref, v_ref, qs_ref, kvs_ref):
    a = _rms(x_ref[...], gin_ref[...])
    proj = _dot(a.astype(bf16), win_ref[...])

    ang = pos_ref[...].astype(f32) * inv_ref[...]
    cos = jnp.cos(ang)
    sin = jnp.sin(ang)

    qs_ref[...] = (proj[:, _C_QS:_C_QS + SWA_WIDTH] * SWA_SCALE).astype(bf16)
    lane = lax.broadcasted_iota(jnp.int32, (proj.shape[0], LANES), 1)
    lo = lane < SWA_HEAD_DIM
    for idx, col in enumerate((_C_KS, _C_VS)):
        t = proj[:, col:col + LANES]
        tsw = pltpu.roll(t, SWA_HEAD_DIM, axis=1)
        base = idx * 4 * LANES
        kvs_ref[:, base + 0 * LANES:base + 1 * LANES] = jnp.where(lo, t, 0.0).astype(bf16)
        kvs_ref[:, base + 1 * LANES:base + 2 * LANES] = jnp.where(lo, 0.0, tsw).astype(bf16)
        kvs_ref[:, base + 2 * LANES:base + 3 * LANES] = jnp.where(lo, tsw, 0.0).astype(bf16)
        kvs_ref[:, base + 3 * LANES:base + 4 * LANES] = jnp.where(lo, 0.0, t).astype(bf16)

    cq = _rms(proj[:, _C_Q:_C_Q + Q_LORA], gq_ref[...])
    qf = _dot(cq.astype(bf16), wq_ref[...])
    ckv = _rms(proj[:, _C_KV:_C_KV + KV_LORA], gkv_ref[...])
    kvf = _dot(ckv.astype(bf16), wkv_ref[...])
    kr = _rope128(proj[:, _C_KR:_C_KR + LANES], cos, sin).astype(bf16)
    for h in range(MLA_HEADS):
        c0 = h * QK_PAD
        q_ref[:, c0:c0 + LANES] = (qf[:, c0:c0 + LANES] * MLA_SCALE).astype(bf16)
        q_ref[:, c0 + LANES:c0 + QK_PAD] = (
            _rope128(qf[:, c0 + LANES:c0 + QK_PAD], cos, sin) * MLA_SCALE).astype(bf16)
        k_ref[:, c0:c0 + LANES] = kvf[:, c0:c0 + LANES].astype(bf16)
        k_ref[:, c0 + LANES:c0 + QK_PAD] = kr
        v_ref[:, h * V_HEAD:(h + 1) * V_HEAD] = kvf[:, c0 + LANES:c0 + QK_PAD].astype(bf16)


def _mla_kernel(q_ref, k_ref, v_ref, o_ref):
    seq = q_ref.shape[0]
    tq = MLA_TQ
    row = lax.broadcasted_iota(jnp.int32, (tq, tq), 0)
    col = lax.broadcasted_iota(jnp.int32, (tq, tq), 1)
    causal = col <= row
    for i in range(seq // tq):
        l0 = i * tq
        q = q_ref[l0:l0 + tq, :]
        sd = jnp.where(causal, _dot_nt(q, k_ref[l0:l0 + tq, :]), -jnp.inf)
        m = jnp.max(sd, axis=1, keepdims=True)
        if i > 0:
            sa = _dot_nt(q, k_ref[0:l0, :])
            m = jnp.maximum(m, jnp.max(sa, axis=1, keepdims=True))
            pa = jnp.exp(sa - m)
        pd = jnp.exp(sd - m)
        den = jnp.sum(pd, axis=1, keepdims=True)
        acc = _dot(pd.astype(bf16), v_ref[l0:l0 + tq, :])
        if i > 0:
            den = den + jnp.sum(pa, axis=1, keepdims=True)
            acc = acc + _dot(pa.astype(bf16), v_ref[0:l0, :])
        o_ref[l0:l0 + tq, :] = (acc / den).astype(o_ref.dtype)


def _swa_bias_table(rel_ref, tab_ref):
    qi = lax.broadcasted_iota(jnp.int32, (BLOCK, 2 * BLOCK), 0) + BLOCK
    kj = lax.broadcasted_iota(jnp.int32, (BLOCK, 2 * BLOCK), 1)
    dist = qi - kj
    in_window = (dist >= 0) & (dist < WINDOW)
    n = jnp.maximum(dist, 0)
    max_exact = REL_BUCKETS // 2
    n_safe = jnp.maximum(n, 1).astype(f32)
    large = max_exact + (jnp.log(n_safe / max_exact) / np.log(REL_MAX_DIST / max_exact)
                         * (REL_BUCKETS - max_exact)).astype(jnp.int32)
    large = jnp.minimum(large, REL_BUCKETS - 1)
    bucket = jnp.where(n < max_exact, n, large)
    for h in range(SWA_Q_HEADS):
        acc = jnp.zeros((BLOCK, 2 * BLOCK), f32)
        for b in range(REL_BUCKETS):
            acc = jnp.where(bucket == b, rel_ref[b, h], acc)
        tab_ref[h] = jnp.where(in_window, acc, -jnp.inf)


def _swa_kernel(rel_ref, sink_ref, q_ref, kvp_ref, kvc_ref, o_ref, tab_ref):
    first = (pl.program_id(0) == 0) & (pl.program_id(1) == 0)

    @pl.when(first)
    def _():
        _swa_bias_table(rel_ref, tab_ref)

    n = pl.program_id(1)
    kj = lax.broadcasted_iota(jnp.int32, (BLOCK, 2 * BLOCK), 1)
    key_ok = kj >= jnp.where(n > 0, 0, BLOCK)
    lane = lax.broadcasted_iota(jnp.int32, (BLOCK, LANES), 1)
    lo = lane < SWA_HEAD_DIM
    band = jnp.concatenate([kvp_ref[...], kvc_ref[...]], axis=0)
    for pair in range(SWA_Q_HEADS // 2):
        kvh = (2 * pair) // SWA_GROUP
        q = q_ref[:, pair * LANES:(pair + 1) * LANES]
        acc = None
        rcp = []
        for sub in range(2):
            h = 2 * pair + sub
            kcol = (2 * kvh + sub) * LANES
            vcol = (4 + 2 * kvh + sub) * LANES
            s = _dot_nt(q, band[:, kcol:kcol + LANES]) + tab_ref[h]
            s = jnp.where(key_ok, s, -jnp.inf)
            sink = sink_ref[0, h]
            m = jnp.maximum(jnp.max(s, axis=1, keepdims=True), sink)
            e = jnp.exp(s - m)
            den = jnp.sum(e, axis=1, keepdims=True) + jnp.exp(sink - m)
            rcp.append(1.0 / den)
            pv = _dot(e.astype(bf16), band[:, vcol:vcol + LANES])
            acc = pv if acc is None else acc + pv
        o_ref[:, pair * LANES:(pair + 1) * LANES] = (
            acc * jnp.where(lo, rcp[0], rcp[1])).astype(o_ref.dtype)


def _out_kernel(x_ref, ya_ref, yb_ref, wa_ref, wb_ref, g_ref, h_ref, m_ref):
    h = x_ref[...] + _dot(ya_ref[...], wa_ref[...]) + _dot(yb_ref[...], wb_ref[...])
    h_ref[...] = h
    m_ref[...] = _rms(h, g_ref[...]).astype(m_ref.dtype)


def _mlp_kernel(m_ref, h_ref, wu_ref, wd_ref, g_ref, o_ref):
    j = pl.program_id(1)
    u = jnp.maximum(_dot(m_ref[...], wu_ref[...]), 0.0)
    part = _dot((u * u).astype(bf16), wd_ref[...])

    @pl.when(j == 0)
    def _():
        o_ref[...] = h_ref[...] + part

    @pl.when(j > 0)
    def _():
        o_ref[...] += part

    @pl.when(j == pl.num_programs(1) - 1)
    def _():
        o_ref[...] = _rms(o_ref[...], g_ref[...])


def _params(sem):
    return pltpu.CompilerParams(dimension_semantics=sem, vmem_limit_bytes=VMEM_LIMIT)


def kernel(x, positions, rel_bias, attn_norm, w_in, q_a_norm, w_q_b, kv_a_norm, w_kv_b, sinks,
           w_out, mlp_norm, w_up, w_down, final_norm):
    batch, seq, d = x.shape
    tokens = batch * seq
    nb = seq // BLOCK
    depth = w_in.shape[0]
    assert depth == 1 and d == D_MODEL and seq % MLA_TQ == 0

    wi = w_in[0]
    c_q, c_kv, k_rope, q_s, k_s, v_s = jnp.split(
        wi, (Q_LORA, Q_LORA + KV_LORA, Q_LORA + KV_LORA + QK_ROPE,
             Q_LORA + KV_LORA + QK_ROPE + SWA_WIDTH,
             Q_LORA + KV_LORA + QK_ROPE + SWA_WIDTH + SWA_KV_WIDTH), axis=1)
    win = jnp.concatenate(
        [c_q, c_kv, q_s, k_s, v_s, k_rope, jnp.zeros((d, LANES - QK_ROPE), wi.dtype)],
        axis=1).astype(bf16)
    wq = jnp.pad(w_q_b[0].reshape(Q_LORA, MLA_HEADS, QK_NOPE + QK_ROPE),
                 ((0, 0), (0, 0), (0, QK_PAD - QK_NOPE - QK_ROPE))
                 ).reshape(Q_LORA, MLA_HEADS * QK_PAD).astype(bf16)
    wkv = w_kv_b[0].astype(bf16)
    wo_a = w_out[0, :MLA_WIDTH].astype(bf16)
    wo_b = w_out[0, MLA_WIDTH:].astype(bf16)
    wu = w_up[0].astype(bf16)
    wd = w_down[0].astype(bf16)

    inv = 1.0 / (ROPE_THETA ** (jnp.arange(0, QK_ROPE, 2, dtype=f32) / QK_ROPE))
    inv128 = jnp.concatenate([inv, inv, jnp.zeros((LANES - QK_ROPE,), f32)])[None, :]

    x2 = x.reshape(tokens, d)
    pos2 = positions.reshape(tokens, 1)
    row = lambda v: v.reshape(1, -1)

    tm = PROJ_TM
    full = lambda shape: pl.BlockSpec(shape, lambda i: (0, 0))
    tok = lambda w: pl.BlockSpec((tm, w), lambda i: (i, 0))
    q, k, v, qs, kvs = pl.pallas_call(
        _proj_kernel,
        grid=(tokens // tm,),
        in_specs=[tok(d), tok(1), full((1, LANES)), full((1, d)), full((d, IN_PAD)),
                  full((1, Q_LORA)), full((Q_LORA, MLA_HEADS * QK_PAD)),
                  full((1, KV_LORA)), full((KV_LORA, MLA_HEADS * (QK_NOPE + V_HEAD)))],
        out_specs=[tok(MLA_HEADS * QK_PAD), tok(MLA_HEADS * QK_PAD), tok(MLA_WIDTH),
                   tok(SWA_WIDTH), tok(8 * LANES)],
        out_shape=[jax.ShapeDtypeStruct((tokens, MLA_HEADS * QK_PAD), bf16),
                   jax.ShapeDtypeStruct((tokens, MLA_HEADS * QK_PAD), bf16),
                   jax.ShapeDtypeStruct((tokens, MLA_WIDTH), bf16),
                   jax.ShapeDtypeStruct((tokens, SWA_WIDTH), bf16),
                   jax.ShapeDtypeStruct((tokens, 8 * LANES), bf16)],
        compiler_params=_params(("arbitrary",)),
        name="proj",
    )(x2, pos2, inv128, row(attn_norm[0]), win, row(q_a_norm[0]), wq, row(kv_a_norm[0]), wkv)

    y_mla = pl.pallas_call(
        _mla_kernel,
        grid=(batch, MLA_HEADS),
        in_specs=[pl.BlockSpec((seq, QK_PAD), lambda b, h: (b, h)),
                  pl.BlockSpec((seq, QK_PAD), lambda b, h: (b, h)),
                  pl.BlockSpec((seq, V_HEAD), lambda b, h: (b, h))],
        out_specs=pl.BlockSpec((seq, V_HEAD), lambda b, h: (b, h)),
        out_shape=jax.ShapeDtypeStruct((tokens, MLA_WIDTH), bf16),
        compiler_params=_params(("arbitrary", "arbitrary")),
        name="mla",
    )(q, k, v)

    smem = pl.BlockSpec(memory_space=pltpu.SMEM)
    y_swa = pl.pallas_call(
        _swa_kernel,
        grid=(batch, nb),
        in_specs=[smem, smem,
                  pl.BlockSpec((BLOCK, SWA_WIDTH), lambda b, n: (b * nb + n, 0)),
                  pl.BlockSpec((BLOCK, 8 * LANES), lambda b, n: (b * nb + jnp.maximum(n - 1, 0), 0)),
                  pl.BlockSpec((BLOCK, 8 * LANES), lambda b, n: (b * nb + n, 0))],
        out_specs=pl.BlockSpec((BLOCK, SWA_WIDTH), lambda b, n: (b * nb + n, 0)),
        out_shape=jax.ShapeDtypeStruct((tokens, SWA_WIDTH), bf16),
        scratch_shapes=[pltpu.VMEM((SWA_Q_HEADS, BLOCK, 2 * BLOCK), f32)],
        compiler_params=_params(("arbitrary", "arbitrary")),
        name="swa",
    )(rel_bias, row(sinks[0]), qs, kvs, kvs)

    tm = OUT_TM
    h1, m = pl.pallas_call(
        _out_kernel,
        grid=(tokens // tm,),
        in_specs=[tok(d), tok(MLA_WIDTH), tok(SWA_WIDTH), full((MLA_WIDTH, d)),
                  full((SWA_WIDTH, d)), full((1, d))],
        out_specs=[tok(d), tok(d)],
        out_shape=[jax.ShapeDtypeStruct((tokens, d), f32),
                   jax.ShapeDtypeStruct((tokens, d), bf16)],
        compiler_params=_params(("arbitrary",)),
        name="outproj",
    )(x2, y_mla, y_swa, wo_a, wo_b, row(mlp_norm[0]))

    tm, th = MLP_TM, MLP_TH
    out = pl.pallas_call(
        _mlp_kernel,
        grid=(tokens // tm, MLP_HIDDEN // th),
        in_specs=[pl.BlockSpec((tm, d), lambda i, j: (i, 0)),
                  pl.BlockSpec((tm, d), lambda i, j: (i, 0)),
                  pl.BlockSpec((d, th), lambda i, j: (0, j)),
                  pl.BlockSpec((th, d), lambda i, j: (j, 0)),
                  pl.BlockSpec((1, d), lambda i, j: (0, 0))],
        out_specs=pl.BlockSpec((tm, d), lambda i, j: (i, 0)),
        out_shape=jax.ShapeDtypeStruct((tokens, d), f32),
        compiler_params=_params(("arbitrary", "arbitrary")),
        name="mlp",
    )(m, h1, wu, wd, row(final_norm))

    return out.reshape(batch, seq, d)
```

```python
import jax
import jax.numpy as jnp
import numpy as np
from jax import lax
from jax.experimental import pallas as pl
from jax.experimental.pallas import tpu as pltpu

D_MODEL = 2048
MLA_HEADS = 8
QK_NOPE = 128
QK_ROPE = 64
V_HEAD = 128
Q_LORA = 512
KV_LORA = 512
ROPE_THETA = 10000.0
SWA_Q_HEADS = 16
SWA_KV_HEADS = 2
SWA_GROUP = SWA_Q_HEADS // SWA_KV_HEADS
SWA_HEAD_DIM = 64
WINDOW = 128
BLOCK = 128
REL_BUCKETS = 32
REL_MAX_DIST = 128
MLP_HIDDEN = 4 * D_MODEL
EPS = 1e-6
MLA_WIDTH = MLA_HEADS * V_HEAD
SWA_WIDTH = SWA_Q_HEADS * SWA_HEAD_DIM
SWA_KV_WIDTH = SWA_KV_HEADS * SWA_HEAD_DIM

LANES = 128
QK_PAD = 2 * LANES
MLA_SCALE = (QK_NOPE + QK_ROPE) ** -0.5
SWA_SCALE = SWA_HEAD_DIM ** -0.5

_C_Q = 0
_C_KV = _C_Q + Q_LORA
_C_QS = _C_KV + KV_LORA
_C_KS = _C_QS + SWA_WIDTH
_C_VS = _C_KS + SWA_KV_WIDTH
_C_KR = _C_VS + SWA_KV_WIDTH
IN_PAD = _C_KR + LANES

VMEM_LIMIT = 56 * 1024 * 1024

PROJ_TM = 512
MLA_TQ = 256
OUT_TM = 512
MLP_TM = 1024
MLP_TH = 512

bf16 = jnp.bfloat16
f32 = jnp.float32


def _rms(x, g):
    return x * lax.rsqrt(jnp.mean(x * x, axis=-1, keepdims=True) + EPS) * g


def _dot(a, b):
    return jnp.dot(a, b, preferred_element_type=f32)


def _dot_nt(a, b):
    return lax.dot_general(a, b, (((1,), (1,)), ((), ())), preferred_element_type=f32)


def _rope128(t, cos, sin):
    lane = lax.broadcasted_iota(jnp.int32, t.shape, 1)
    fwd = pltpu.roll(t, QK_ROPE // 2, axis=1)
    bwd = pltpu.roll(t, LANES - QK_ROPE // 2, axis=1)
    rot = jnp.where((lane & (QK_ROPE // 2)) == 0, -bwd, fwd)
    return t * cos + rot * sin


def _proj_kernel(x_ref, pos_ref, inv_ref, gin_ref, win_ref, gq_ref, wq_ref, gkv_ref, wkv_ref,
                 q_ref, k_ref, v_ref, qs_ref, kvs_ref):
    a = _rms(x_ref[...], gin_ref[...])
    proj = _dot(a.astype(bf16), win_ref[...])

    ang = pos_ref[...].astype(f32) * inv_ref[...]
    cos = jnp.cos(ang)
    sin = jnp.sin(ang)

    qs_ref[...] = (proj[:, _C_QS:_C_QS + SWA_WIDTH] * SWA_SCALE).astype(bf16)
    lane = lax.broadcasted_iota(jnp.int32, (proj.shape[0], LANES), 1)
    lo = lane < SWA_HEAD_DIM
    for idx, col in enumerate((_C_KS, _C_VS)):
        t = proj[:, col:col + LANES]
        tsw = pltpu.roll(t, SWA_HEAD_DIM, axis=1)
        base = idx * 4 * LANES
        kvs_ref[:, base + 0 * LANES:base + 1 * LANES] = jnp.where(lo, t, 0.0).astype(bf16)
        kvs_ref[:, base + 1 * LANES:base + 2 * LANES] = jnp.where(lo, 0.0, tsw).astype(bf16)
        kvs_ref[:, base + 2 * LANES:base + 3 * LANES] = jnp.where(lo, tsw, 0.0).astype(bf16)
        kvs_ref[:, base + 3 * LANES:base + 4 * LANES] = jnp.where(lo, 0.0, t).astype(bf16)

    cq = _rms(proj[:, _C_Q:_C_Q + Q_LORA], gq_ref[...])
    qf = _dot(cq.astype(bf16), wq_ref[...])
    ckv = _rms(proj[:, _C_KV:_C_KV + KV_LORA], gkv_ref[...])
    kvf = _dot(ckv.astype(bf16), wkv_ref[...])
    kr = _rope128(proj[:, _C_KR:_C_KR + LANES], cos, sin).astype(bf16)
    for h in range(MLA_HEADS):
        c0 = h * QK_PAD
        q_ref[:, c0:c0 + LANES] = (qf[:, c0:c0 + LANES] * MLA_SCALE).astype(bf16)
        q_ref[:, c0 + LANES:c0 + QK_PAD] = (
            _rope128(qf[:, c0 + LANES:c0 + QK_PAD], cos, sin) * MLA_SCALE).astype(bf16)
        k_ref[:, c0:c0 + LANES] = kvf[:, c0:c0 + LANES].astype(bf16)
        k_ref[:, c0 + LANES:c0 + QK_PAD] = kr
        v_ref[:, h * V_HEAD:(h + 1) * V_HEAD] = kvf[:, c0 + LANES:c0 + QK_PAD].astype(bf16)


def _mla_kernel(q_ref, k_ref, v_ref, o_ref):
    seq = q_ref.shape[0]
    tq = MLA_TQ
    row = lax.broadcasted_iota(jnp.int32, (tq, tq), 0)
    col = lax.broadcasted_iota(jnp.int32, (tq, tq), 1)
    causal = col <= row
    for i in range(seq // tq):
        l0 = i * tq
        q = q_ref[l0:l0 + tq, :]
        sd = jnp.where(causal, _dot_nt(q, k_ref[l0:l0 + tq, :]), -jnp.inf)
        m = jnp.max(sd, axis=1, keepdims=True)
        if i > 0:
            sa = _dot_nt(q, k_ref[0:l0, :])
            m = jnp.maximum(m, jnp.max(sa, axis=1, keepdims=True))
            pa = jnp.exp(sa - m)
        pd = jnp.exp(sd - m)
        den = jnp.sum(pd, axis=1, keepdims=True)
        acc = _dot(pd.astype(bf16), v_ref[l0:l0 + tq, :])
        if i > 0:
            den = den + jnp.sum(pa, axis=1, keepdims=True)
            acc = acc + _dot(pa.astype(bf16), v_ref[0:l0, :])
        o_ref[l0:l0 + tq, :] = (acc / den).astype(o_ref.dtype)


def _swa_bias_table(rel_ref, tab_ref):
    qi = lax.broadcasted_iota(jnp.int32, (BLOCK, 2 * BLOCK), 0) + BLOCK
    kj = lax.broadcasted_iota(jnp.int32, (BLOCK, 2 * BLOCK), 1)
    dist = qi - kj
    in_window = (dist >= 0) & (dist < WINDOW)
    n = jnp.maximum(dist, 0)
    max_exact = REL_BUCKETS // 2
    n_safe = jnp.maximum(n, 1).astype(f32)
    large = max_exact + (jnp.log(n_safe / max_exact) / np.log(REL_MAX_DIST / max_exact)
                         * (REL_BUCKETS - max_exact)).astype(jnp.int32)
    large = jnp.minimum(large, REL_BUCKETS - 1)
    bucket = jnp.where(n < max_exact, n, large)
    for h in range(SWA_Q_HEADS):
        acc = jnp.zeros((BLOCK, 2 * BLOCK), f32)
        for b in range(REL_BUCKETS):
            acc = jnp.where(bucket == b, rel_ref[b, h], acc)
        tab_ref[h] = jnp.where(in_window, acc, -jnp.inf)


def _swa_kernel(rel_ref, sink_ref, q_ref, kvp_ref, kvc_ref, o_ref, tab_ref):
    first = (pl.program_id(0) == 0) & (pl.program_id(1) == 0)

    @pl.when(first)
    def _():
        _swa_bias_table(rel_ref, tab_ref)

    n = pl.program_id(1)
    kj = lax.broadcasted_iota(jnp.int32, (BLOCK, 2 * BLOCK), 1)
    key_ok = kj >= jnp.where(n > 0, 0, BLOCK)
    lane = lax.broadcasted_iota(jnp.int32, (BLOCK, LANES), 1)
    lo = lane < SWA_HEAD_DIM
    band = jnp.concatenate([kvp_ref[...], kvc_ref[...]], axis=0)
    scores = []
    for h in range(SWA_Q_HEADS):
        pair, sub = divmod(h, 2)
        kcol = (2 * (h // SWA_GROUP) + sub) * LANES
        s = _dot_nt(q_ref[:, pair * LANES:(pair + 1) * LANES], band[:, kcol:kcol + LANES])
        scores.append(jnp.where(key_ok, s + tab_ref[h], -jnp.inf))
    probs, rcps = [], []
    for h in range(SWA_Q_HEADS):
        sink = sink_ref[0, h]
        m = jnp.maximum(jnp.max(scores[h], axis=1, keepdims=True), sink)
        e = jnp.exp(scores[h] - m)
        rcps.append(1.0 / (jnp.sum(e, axis=1, keepdims=True) + jnp.exp(sink - m)))
        probs.append(e.astype(bf16))
    for pair in range(SWA_Q_HEADS // 2):
        acc = None
        for sub in range(2):
            h = 2 * pair + sub
            vcol = (4 + 2 * (h // SWA_GROUP) + sub) * LANES
            pv = _dot(probs[h], band[:, vcol:vcol + LANES])
            acc = pv if acc is None else acc + pv
        o_ref[:, pair * LANES:(pair + 1) * LANES] = (
            acc * jnp.where(lo, rcps[2 * pair], rcps[2 * pair + 1])).astype(o_ref.dtype)


def _out_kernel(x_ref, ya_ref, yb_ref, wa_ref, wb_ref, g_ref, h_ref, m_ref):
    h = x_ref[...] + _dot(ya_ref[...], wa_ref[...]) + _dot(yb_ref[...], wb_ref[...])
    h_ref[...] = h
    m_ref[...] = _rms(h, g_ref[...]).astype(m_ref.dtype)


def _mlp_kernel(m_ref, h_ref, wu_ref, wd_ref, g_ref, o_ref):
    j = pl.program_id(1)

    @pl.when(j == 0)
    def _():
        o_ref[...] = h_ref[...]

    u = jnp.maximum(_dot(m_ref[...], wu_ref[...]), 0.0)
    o_ref[...] += _dot((u * u).astype(bf16), wd_ref[...])

    @pl.when(j == pl.num_programs(1) - 1)
    def _():
        o_ref[...] = _rms(o_ref[...], g_ref[...])


def _params(sem):
    return pltpu.CompilerParams(dimension_semantics=sem, vmem_limit_bytes=VMEM_LIMIT)


def kernel(x, positions, rel_bias, attn_norm, w_in, q_a_norm, w_q_b, kv_a_norm, w_kv_b, sinks,
           w_out, mlp_norm, w_up, w_down, final_norm):
    batch, seq, d = x.shape
    tokens = batch * seq
    nb = seq // BLOCK
    depth = w_in.shape[0]
    assert depth == 1 and d == D_MODEL and seq % MLA_TQ == 0

    wi = w_in[0]
    c_q, c_kv, k_rope, q_s, k_s, v_s = jnp.split(
        wi, (Q_LORA, Q_LORA + KV_LORA, Q_LORA + KV_LORA + QK_ROPE,
             Q_LORA + KV_LORA + QK_ROPE + SWA_WIDTH,
             Q_LORA + KV_LORA + QK_ROPE + SWA_WIDTH + SWA_KV_WIDTH), axis=1)
    win = jnp.concatenate(
        [c_q, c_kv, q_s, k_s, v_s, k_rope, jnp.zeros((d, LANES - QK_ROPE), wi.dtype)],
        axis=1).astype(bf16)
    wq = jnp.pad(w_q_b[0].reshape(Q_LORA, MLA_HEADS, QK_NOPE + QK_ROPE),
                 ((0, 0), (0, 0), (0, QK_PAD - QK_NOPE - QK_ROPE))
                 ).reshape(Q_LORA, MLA_HEADS * QK_PAD).astype(bf16)
    wkv = w_kv_b[0].astype(bf16)
    wo_a = w_out[0, :MLA_WIDTH].astype(bf16)
    wo_b = w_out[0, MLA_WIDTH:].astype(bf16)
    wu = w_up[0].astype(bf16)
    wd = w_down[0].astype(bf16)

    inv = 1.0 / (ROPE_THETA ** (jnp.arange(0, QK_ROPE, 2, dtype=f32) / QK_ROPE))
    inv128 = jnp.concatenate([inv, inv, jnp.zeros((LANES - QK_ROPE,), f32)])[None, :]

    x2 = x.reshape(tokens, d)
    pos2 = positions.reshape(tokens, 1)
    row = lambda v: v.reshape(1, -1)

    tm = PROJ_TM
    full = lambda shape: pl.BlockSpec(shape, lambda i: (0, 0))
    tok = lambda w: pl.BlockSpec((tm, w), lambda i: (i, 0))
    q, k, v, qs, kvs = pl.pallas_call(
        _proj_kernel,
        grid=(tokens // tm,),
        in_specs=[tok(d), tok(1), full((1, LANES)), full((1, d)), full((d, IN_PAD)),
                  full((1, Q_LORA)), full((Q_LORA, MLA_HEADS * QK_PAD)),
                  full((1, KV_LORA)), full((KV_LORA, MLA_HEADS * (QK_NOPE + V_HEAD)))],
        out_specs=[tok(MLA_HEADS * QK_PAD), tok(MLA_HEADS * QK_PAD), tok(MLA_WIDTH),
                   tok(SWA_WIDTH), tok(8 * LANES)],
        out_shape=[jax.ShapeDtypeStruct((tokens, MLA_HEADS * QK_PAD), bf16),
                   jax.ShapeDtypeStruct((tokens, MLA_HEADS * QK_PAD), bf16),
                   jax.ShapeDtypeStruct((tokens, MLA_WIDTH), bf16),
                   jax.ShapeDtypeStruct((tokens, SWA_WIDTH), bf16),
                   jax.ShapeDtypeStruct((tokens, 8 * LANES), bf16)],
        compiler_params=_params(("arbitrary",)),
        name="proj",
    )(x2, pos2, inv128, row(attn_norm[0]), win, row(q_a_norm[0]), wq, row(kv_a_norm[0]), wkv)

    y_mla = pl.pallas_call(
        _mla_kernel,
        grid=(batch, MLA_HEADS),
        in_specs=[pl.BlockSpec((seq, QK_PAD), lambda b, h: (b, h)),
                  pl.BlockSpec((seq, QK_PAD), lambda b, h: (b, h)),
                  pl.BlockSpec((seq, V_HEAD), lambda b, h: (b, h))],
        out_specs=pl.BlockSpec((seq, V_HEAD), lambda b, h: (b, h)),
        out_shape=jax.ShapeDtypeStruct((tokens, MLA_WIDTH), bf16),
        compiler_params=_params(("arbitrary", "arbitrary")),
        name="mla",
    )(q, k, v)

    smem = pl.BlockSpec(memory_space=pltpu.SMEM)
    y_swa = pl.pallas_call(
        _swa_kernel,
        grid=(batch, nb),
        in_specs=[smem, smem,
                  pl.BlockSpec((BLOCK, SWA_WIDTH), lambda b, n: (b * nb + n, 0)),
                  pl.BlockSpec((BLOCK, 8 * LANES), lambda b, n: (b * nb + jnp.maximum(n - 1, 0), 0)),
                  pl.BlockSpec((BLOCK, 8 * LANES), lambda b, n: (b * nb + n, 0))],
        out_specs=pl.BlockSpec((BLOCK, SWA_WIDTH), lambda b, n: (b * nb + n, 0)),
        out_shape=jax.ShapeDtypeStruct((tokens, SWA_WIDTH), bf16),
        scratch_shapes=[pltpu.VMEM((SWA_Q_HEADS, BLOCK, 2 * BLOCK), f32)],
        compiler_params=_params(("arbitrary", "arbitrary")),
        name="swa",
    )(rel_bias, row(sinks[0]), qs, kvs, kvs)

    tm = OUT_TM
    h1, m = pl.pallas_call(
        _out_kernel,
        grid=(tokens // tm,),
        in_specs=[tok(d), tok(MLA_WIDTH), tok(SWA_WIDTH), full((MLA_WIDTH, d)),
                  full((SWA_WIDTH, d)), full((1, d))],
        out_specs=[tok(d), tok(d)],
        out_shape=[jax.ShapeDtypeStruct((tokens, d), f32),
                   jax.ShapeDtypeStruct((tokens, d), bf16)],
        compiler_params=_params(("arbitrary",)),
        name="outproj",
    )(x2, y_mla, y_swa, wo_a, wo_b, row(mlp_norm[0]))

    tm, th = MLP_TM, MLP_TH
    out = pl.pallas_call(
        _mlp_kernel,
        grid=(tokens // tm, MLP_HIDDEN // th),
        in_specs=[pl.BlockSpec((tm, d), lambda i, j: (i, 0)),
                  pl.BlockSpec((tm, d), lambda i, j: (i, 0)),
                  pl.BlockSpec((d, th), lambda i, j: (0, j)),
                  pl.BlockSpec((th, d), lambda i, j: (j, 0)),
                  pl.BlockSpec((1, d), lambda i, j: (0, 0))],
        out_specs=pl.BlockSpec((tm, d), lambda i, j: (i, 0)),
        out_shape=jax.ShapeDtypeStruct((tokens, d), f32),
        compiler_params=_params(("arbitrary", "arbitrary")),
        name="mlp",
    )(m, h1, wu, wd, row(final_norm))

    return out.reshape(batch, seq, d)
```

```python
import jax
import jax.numpy as jnp
import numpy as np
from jax import lax
from jax.experimental import pallas as pl
from jax.experimental.pallas import tpu as pltpu

D_MODEL = 2048
MLA_HEADS = 8
QK_NOPE = 128
QK_ROPE = 64
V_HEAD = 128
Q_LORA = 512
KV_LORA = 512
ROPE_THETA = 10000.0
SWA_Q_HEADS = 16
SWA_KV_HEADS = 2
SWA_GROUP = SWA_Q_HEADS // SWA_KV_HEADS
SWA_HEAD_DIM = 64
WINDOW = 128
BLOCK = 128
REL_BUCKETS = 32
REL_MAX_DIST = 128
MLP_HIDDEN = 4 * D_MODEL
EPS = 1e-6
MLA_WIDTH = MLA_HEADS * V_HEAD
SWA_WIDTH = SWA_Q_HEADS * SWA_HEAD_DIM
SWA_KV_WIDTH = SWA_KV_HEADS * SWA_HEAD_DIM

LANES = 128
QK_PAD = 2 * LANES
MLA_SCALE = (QK_NOPE + QK_ROPE) ** -0.5 * float(np.log2(np.e))
SWA_SCALE = SWA_HEAD_DIM ** -0.5

_C_Q = 0
_C_KV = _C_Q + Q_LORA
_C_QS = _C_KV + KV_LORA
_C_KS = _C_QS + SWA_WIDTH
_C_VS = _C_KS + SWA_KV_WIDTH
_C_KR = _C_VS + SWA_KV_WIDTH
IN_PAD = _C_KR + LANES

VMEM_LIMIT = 56 * 1024 * 1024

PROJ_TM = 512
MLA_TQ = 256
OUT_TM = 512
MLP_TM = 1024
MLP_TH = 512

bf16 = jnp.bfloat16
f32 = jnp.float32


def _rms(x, g):
    return x * lax.rsqrt(jnp.mean(x * x, axis=-1, keepdims=True) + EPS) * g


def _dot(a, b):
    return jnp.dot(a, b, preferred_element_type=f32)


def _dot_nt(a, b):
    return lax.dot_general(a, b, (((1,), (1,)), ((), ())), preferred_element_type=f32)


def _dot_tn(a, b):
    return lax.dot_general(a, b, (((0,), (0,)), ((), ())), preferred_element_type=f32)


def _rope128(t, cos, sin):
    lane = lax.broadcasted_iota(jnp.int32, t.shape, 1)
    fwd = pltpu.roll(t, QK_ROPE // 2, axis=1)
    bwd = pltpu.roll(t, LANES - QK_ROPE // 2, axis=1)
    rot = jnp.where((lane & (QK_ROPE // 2)) == 0, -bwd, fwd)
    return t * cos + rot * sin


def _proj_kernel(x_ref, pos_ref, inv_ref, gin_ref, win_ref, gq_ref, wq_ref, gkv_ref, wkv_ref,
                 q_ref, k_ref, v_ref, qs_ref, kvs_ref):
    a = _rms(x_ref[...], gin_ref[...])
    proj = _dot(a.astype(bf16), win_ref[...])

    ang = pos_ref[...].astype(f32) * inv_ref[...]
    cos = jnp.cos(ang)
    sin = jnp.sin(ang)

    qs_ref[...] = (proj[:, _C_QS:_C_QS + SWA_WIDTH] * SWA_SCALE).astype(bf16)
    lane = lax.broadcasted_iota(jnp.int32, (proj.shape[0], LANES), 1)
    lo = lane < SWA_HEAD_DIM
    for idx, col in enumerate((_C_KS, _C_VS)):
        t = proj[:, col:col + LANES]
        tsw = pltpu.roll(t, SWA_HEAD_DIM, axis=1)
        base = idx * 4 * LANES
        kvs_ref[:, base + 0 * LANES:base + 1 * LANES] = jnp.where(lo, t, 0.0).astype(bf16)
        kvs_ref[:, base + 1 * LANES:base + 2 * LANES] = jnp.where(lo, 0.0, tsw).astype(bf16)
        kvs_ref[:, base + 2 * LANES:base + 3 * LANES] = jnp.where(lo, tsw, 0.0).astype(bf16)
        kvs_ref[:, base + 3 * LANES:base + 4 * LANES] = jnp.where(lo, 0.0, t).astype(bf16)

    cq = _rms(proj[:, _C_Q:_C_Q + Q_LORA], gq_ref[...])
    qf = _dot(cq.astype(bf16), wq_ref[...])
    ckv = _rms(proj[:, _C_KV:_C_KV + KV_LORA], gkv_ref[...])
    kvf = _dot(ckv.astype(bf16), wkv_ref[...])
    kr = _rope128(proj[:, _C_KR:_C_KR + LANES], cos, sin).astype(bf16)
    for h in range(MLA_HEADS):
        c0 = h * QK_PAD
        q_ref[:, c0:c0 + LANES] = (qf[:, c0:c0 + LANES] * MLA_SCALE).astype(bf16)
        q_ref[:, c0 + LANES:c0 + QK_PAD] = (
            _rope128(qf[:, c0 + LANES:c0 + QK_PAD], cos, sin) * MLA_SCALE).astype(bf16)
        k_ref[:, c0:c0 + LANES] = kvf[:, c0:c0 + LANES].astype(bf16)
        k_ref[:, c0 + LANES:c0 + QK_PAD] = kr
        v_ref[:, h * V_HEAD:(h + 1) * V_HEAD] = kvf[:, c0 + LANES:c0 + QK_PAD].astype(bf16)


def _mla_kernel(q_ref, k_ref, v_ref, wu_ref, wd_ref, o_ref, wu_o_ref, wd_o_ref, s_ref, vt_ref):
    wu_o_ref[...] = wu_ref[...].astype(bf16)
    wd_o_ref[...] = wd_ref[...].astype(bf16)

    seq = q_ref.shape[0]
    tq = MLA_TQ
    n_tiles = seq // tq

    def scores(i):
        l = (i + 1) * tq
        s_ref[i % 2, 0:l, :] = _dot_nt(k_ref[0:l, :], q_ref[i * tq:(i + 1) * tq, :])

    vt_ref[0:V_HEAD, :] = v_ref[...].T
    vt_ref[V_HEAD:, :] = jnp.ones((vt_ref.shape[0] - V_HEAD, seq), bf16)
    scores(0)
    for i in range(n_tiles):
        if i + 1 < n_tiles:
            scores(i + 1)
        l0 = i * tq
        key = lax.broadcasted_iota(jnp.int32, (tq, tq), 0)
        qry = lax.broadcasted_iota(jnp.int32, (tq, tq), 1)
        sd = jnp.where(key <= qry, s_ref[i % 2, l0:l0 + tq, :], -jnp.inf)
        m = jnp.max(sd, axis=0, keepdims=True)
        if i > 0:
            sa = s_ref[i % 2, 0:l0, :]
            m = jnp.maximum(m, jnp.max(sa, axis=0, keepdims=True))
        acc = _dot(vt_ref[:, l0:l0 + tq], jnp.exp2(sd - m).astype(bf16))
        if i > 0:
            acc = acc + _dot(vt_ref[:, 0:l0], jnp.exp2(sa - m).astype(bf16))
        o_ref[l0:l0 + tq, :] = (acc[0:V_HEAD] / acc[V_HEAD:V_HEAD + 1]).T.astype(o_ref.dtype)


def _swa_bias_table(rel_ref, tab_ref):
    qi = lax.broadcasted_iota(jnp.int32, (BLOCK, 2 * BLOCK), 0) + BLOCK
    kj = lax.broadcasted_iota(jnp.int32, (BLOCK, 2 * BLOCK), 1)
    dist = qi - kj
    in_window = (dist >= 0) & (dist < WINDOW)
    n = jnp.maximum(dist, 0)
    max_exact = REL_BUCKETS // 2
    n_safe = jnp.maximum(n, 1).astype(f32)
    large = max_exact + (jnp.log(n_safe / max_exact) / np.log(REL_MAX_DIST / max_exact)
                         * (REL_BUCKETS - max_exact)).astype(jnp.int32)
    large = jnp.minimum(large, REL_BUCKETS - 1)
    bucket = jnp.where(n < max_exact, n, large)
    for h in range(SWA_Q_HEADS):
        acc = jnp.zeros((BLOCK, 2 * BLOCK), f32)
        for b in range(REL_BUCKETS):
            acc = jnp.where(bucket == b, rel_ref[b, h], acc)
        tab_ref[h] = jnp.where(in_window, acc, -jnp.inf)


def _swa_kernel(rel_ref, sink_ref, q_ref, kvp_ref, kvc_ref, wo_ref, o_ref, wo_o_ref, tab_ref):
    wo_o_ref[...] = wo_ref[...].astype(bf16)
    first = (pl.program_id(0) == 0) & (pl.program_id(1) == 0)

    @pl.when(first)
    def _():
        _swa_bias_table(rel_ref, tab_ref)

    n = pl.program_id(1)
    kj = lax.broadcasted_iota(jnp.int32, (BLOCK, 2 * BLOCK), 1)
    key_ok = kj >= jnp.where(n > 0, 0, BLOCK)
    lane = lax.broadcasted_iota(jnp.int32, (BLOCK, LANES), 1)
    lo = lane < SWA_HEAD_DIM
    band = jnp.concatenate([kvp_ref[...], kvc_ref[...]], axis=0)
    scores = []
    for h in range(SWA_Q_HEADS):
        pair, sub = divmod(h, 2)
        kcol = (2 * (h // SWA_GROUP) + sub) * LANES
        s = _dot_nt(q_ref[:, pair * LANES:(pair + 1) * LANES], band[:, kcol:kcol + LANES])
        scores.append(jnp.where(key_ok, s + tab_ref[h], -jnp.inf))
    probs, rcps = [], []
    for h in range(SWA_Q_HEADS):
        sink = sink_ref[0, h]
        m = jnp.maximum(jnp.max(scores[h], axis=1, keepdims=True), sink)
        e = jnp.exp(scores[h] - m)
        rcps.append(1.0 / (jnp.sum(e, axis=1, keepdims=True) + jnp.exp(sink - m)))
        probs.append(e.astype(bf16))
    for pair in range(SWA_Q_HEADS // 2):
        acc = None
        for sub in range(2):
            h = 2 * pair + sub
            vcol = (4 + 2 * (h // SWA_GROUP) + sub) * LANES
            pv = _dot(probs[h], band[:, vcol:vcol + LANES])
            acc = pv if acc is None else acc + pv
        o_ref[:, pair * LANES:(pair + 1) * LANES] = (
            acc * jnp.where(lo, rcps[2 * pair], rcps[2 * pair + 1])).astype(o_ref.dtype)


def _out_kernel(x_ref, ya_ref, yb_ref, wa_ref, wb_ref, g_ref, h_ref, m_ref):
    h = x_ref[...] + _dot(ya_ref[...], wa_ref[...]) + _dot(yb_ref[...], wb_ref[...])
    h_ref[...] = h
    m_ref[...] = _rms(h, g_ref[...]).astype(m_ref.dtype)


def _mlp_kernel(m_ref, h_ref, wu_ref, wd_ref, g_ref, o_ref):
    j = pl.program_id(1)

    @pl.when(j == 0)
    def _():
        o_ref[...] = h_ref[...]

    u = jnp.maximum(_dot(m_ref[...], wu_ref[...]), 0.0)
    o_ref[...] += _dot((u * u).astype(bf16), wd_ref[...])

    @pl.when(j == pl.num_programs(1) - 1)
    def _():
        o_ref[...] = _rms(o_ref[...], g_ref[...])


def _params(sem):
    return pltpu.CompilerParams(dimension_semantics=sem, vmem_limit_bytes=VMEM_LIMIT)


def kernel(x, positions, rel_bias, attn_norm, w_in, q_a_norm, w_q_b, kv_a_norm, w_kv_b, sinks,
           w_out, mlp_norm, w_up, w_down, final_norm):
    batch, seq, d = x.shape
    tokens = batch * seq
    nb = seq // BLOCK
    depth = w_in.shape[0]
    assert depth == 1 and d == D_MODEL and seq % MLA_TQ == 0

    wi = w_in[0]
    c_q, c_kv, k_rope, q_s, k_s, v_s = jnp.split(
        wi, (Q_LORA, Q_LORA + KV_LORA, Q_LORA + KV_LORA + QK_ROPE,
             Q_LORA + KV_LORA + QK_ROPE + SWA_WIDTH,
             Q_LORA + KV_LORA + QK_ROPE + SWA_WIDTH + SWA_KV_WIDTH), axis=1)
    win = jnp.concatenate(
        [c_q, c_kv, q_s, k_s, v_s, k_rope, jnp.zeros((d, LANES - QK_ROPE), wi.dtype)],
        axis=1).astype(bf16)
    wq = jnp.pad(w_q_b[0].reshape(Q_LORA, MLA_HEADS, QK_NOPE + QK_ROPE),
                 ((0, 0), (0, 0), (0, QK_PAD - QK_NOPE - QK_ROPE))
                 ).reshape(Q_LORA, MLA_HEADS * QK_PAD).astype(bf16)
    wkv = w_kv_b[0].astype(bf16)

    inv = 1.0 / (ROPE_THETA ** (jnp.arange(0, QK_ROPE, 2, dtype=f32) / QK_ROPE))
    inv128 = jnp.concatenate([inv, inv, jnp.zeros((LANES - QK_ROPE,), f32)])[None, :]

    x2 = x.reshape(tokens, d)
    pos2 = positions.reshape(tokens, 1)
    row = lambda v: v.reshape(1, -1)

    tm = PROJ_TM
    full = lambda shape: pl.BlockSpec(shape, lambda i: (0, 0))
    tok = lambda w: pl.BlockSpec((tm, w), lambda i: (i, 0))
    q, k, v, qs, kvs = pl.pallas_call(
        _proj_kernel,
        grid=(tokens // tm,),
        in_specs=[tok(d), tok(1), full((1, LANES)), full((1, d)), full((d, IN_PAD)),
                  full((1, Q_LORA)), full((Q_LORA, MLA_HEADS * QK_PAD)),
                  full((1, KV_LORA)), full((KV_LORA, MLA_HEADS * (QK_NOPE + V_HEAD)))],
        out_specs=[tok(MLA_HEADS * QK_PAD), tok(MLA_HEADS * QK_PAD), tok(MLA_WIDTH),
                   tok(SWA_WIDTH), tok(8 * LANES)],
        out_shape=[jax.ShapeDtypeStruct((tokens, MLA_HEADS * QK_PAD), bf16),
                   jax.ShapeDtypeStruct((tokens, MLA_HEADS * QK_PAD), bf16),
                   jax.ShapeDtypeStruct((tokens, MLA_WIDTH), bf16),
                   jax.ShapeDtypeStruct((tokens, SWA_WIDTH), bf16),
                   jax.ShapeDtypeStruct((tokens, 8 * LANES), bf16)],
        compiler_params=_params(("arbitrary",)),
        name="proj",
    )(x2, pos2, inv128, row(attn_norm[0]), win, row(q_a_norm[0]), wq, row(kv_a_norm[0]), wkv)

    mla_steps = batch * MLA_HEADS
    wu_rows = d // mla_steps
    wd_rows = MLP_HIDDEN // mla_steps
    slab = lambda rows, cols: pl.BlockSpec((rows, cols), lambda b, h: (b * MLA_HEADS + h, 0))
    y_mla, wu, wd = pl.pallas_call(
        _mla_kernel,
        grid=(batch, MLA_HEADS),
        in_specs=[pl.BlockSpec((seq, QK_PAD), lambda b, h: (b, h)),
                  pl.BlockSpec((seq, QK_PAD), lambda b, h: (b, h)),
                  pl.BlockSpec((seq, V_HEAD), lambda b, h: (b, h)),
                  slab(wu_rows, MLP_HIDDEN), slab(wd_rows, d)],
        out_specs=[pl.BlockSpec((seq, V_HEAD), lambda b, h: (b, h)),
                   slab(wu_rows, MLP_HIDDEN), slab(wd_rows, d)],
        out_shape=[jax.ShapeDtypeStruct((tokens, MLA_WIDTH), bf16),
                   jax.ShapeDtypeStruct((d, MLP_HIDDEN), bf16),
                   jax.ShapeDtypeStruct((MLP_HIDDEN, d), bf16)],
        scratch_shapes=[pltpu.VMEM((2, seq, MLA_TQ), f32), pltpu.VMEM((V_HEAD + 16, seq), bf16)],
        compiler_params=_params(("arbitrary", "arbitrary")),
        name="mla",
    )(q, k, v, w_up[0], w_down[0])

    smem = pl.BlockSpec(memory_space=pltpu.SMEM)
    wo_rows = (MLA_WIDTH + SWA_WIDTH) // (batch * nb)
    wo_slab = pl.BlockSpec((wo_rows, d), lambda b, n: (b * nb + n, 0))
    y_swa, wo = pl.pallas_call(
        _swa_kernel,
        grid=(batch, nb),
        in_specs=[smem, smem,
                  pl.BlockSpec((BLOCK, SWA_WIDTH), lambda b, n: (b * nb + n, 0)),
                  pl.BlockSpec((BLOCK, 8 * LANES), lambda b, n: (b * nb + jnp.maximum(n - 1, 0), 0)),
                  pl.BlockSpec((BLOCK, 8 * LANES), lambda b, n: (b * nb + n, 0)),
                  wo_slab],
        out_specs=[pl.BlockSpec((BLOCK, SWA_WIDTH), lambda b, n: (b * nb + n, 0)), wo_slab],
        out_shape=[jax.ShapeDtypeStruct((tokens, SWA_WIDTH), bf16),
                   jax.ShapeDtypeStruct((MLA_WIDTH + SWA_WIDTH, d), bf16)],
        scratch_shapes=[pltpu.VMEM((SWA_Q_HEADS, BLOCK, 2 * BLOCK), f32)],
        compiler_params=_params(("arbitrary", "arbitrary")),
        name="swa",
    )(rel_bias, row(sinks[0]), qs, kvs, kvs, w_out[0])

    tm = OUT_TM
    h1, m = pl.pallas_call(
        _out_kernel,
        grid=(tokens // tm,),
        in_specs=[tok(d), tok(MLA_WIDTH), tok(SWA_WIDTH),
                  pl.BlockSpec((MLA_WIDTH, d), lambda i: (0, 0)),
                  pl.BlockSpec((SWA_WIDTH, d), lambda i: (1, 0)), full((1, d))],
        out_specs=[tok(d), tok(d)],
        out_shape=[jax.ShapeDtypeStruct((tokens, d), f32),
                   jax.ShapeDtypeStruct((tokens, d), bf16)],
        compiler_params=_params(("arbitrary",)),
        name="outproj",
    )(x2, y_mla, y_swa, wo, wo, row(mlp_norm[0]))

    tm, th = MLP_TM, MLP_TH
    out = pl.pallas_call(
        _mlp_kernel,
        grid=(tokens // tm, MLP_HIDDEN // th),
        in_specs=[pl.BlockSpec((tm, d), lambda i, j: (i, 0)),
                  pl.BlockSpec((tm, d), lambda i, j: (i, 0)),
                  pl.BlockSpec((d, th), lambda i, j: (0, j)),
                  pl.BlockSpec((th, d), lambda i, j: (j, 0)),
                  pl.BlockSpec((1, d), lambda i, j: (0, 0))],
        out_specs=pl.BlockSpec((tm, d), lambda i, j: (i, 0)),
        out_shape=jax.ShapeDtypeStruct((tokens, d), f32),
        compiler_params=_params(("arbitrary", "arbitrary")),
        name="mlp",
    )(m, h1, wu, wd, row(final_norm))

    return out.reshape(batch, seq, d)
```

```python
import jax
import jax.numpy as jnp
import numpy as np
from jax import lax
from jax.experimental import pallas as pl
from jax.experimental.pallas import tpu as pltpu

D_MODEL = 2048
MLA_HEADS = 8
QK_NOPE = 128
QK_ROPE = 64
V_HEAD = 128
Q_LORA = 512
KV_LORA = 512
ROPE_THETA = 10000.0
SWA_Q_HEADS = 16
SWA_KV_HEADS = 2
SWA_GROUP = SWA_Q_HEADS // SWA_KV_HEADS
SWA_HEAD_DIM = 64
WINDOW = 128
BLOCK = 128
REL_BUCKETS = 32
REL_MAX_DIST = 128
MLP_HIDDEN = 4 * D_MODEL
EPS = 1e-6
MLA_WIDTH = MLA_HEADS * V_HEAD
SWA_WIDTH = SWA_Q_HEADS * SWA_HEAD_DIM
SWA_KV_WIDTH = SWA_KV_HEADS * SWA_HEAD_DIM

LANES = 128
QK_PAD = 2 * LANES
MLA_SCALE = (QK_NOPE + QK_ROPE) ** -0.5 * float(np.log2(np.e))
SWA_SCALE = SWA_HEAD_DIM ** -0.5

_C_Q = 0
_C_KV = _C_Q + Q_LORA
_C_QS = _C_KV + KV_LORA
_C_KS = _C_QS + SWA_WIDTH
_C_VS = _C_KS + SWA_KV_WIDTH
_C_KR = _C_VS + SWA_KV_WIDTH
IN_PAD = _C_KR + LANES

VMEM_LIMIT = 56 * 1024 * 1024

PROJ_TM = 512
MLA_TQ = 256
OUT_TM = 512
MLP_TM = 1024
MLP_TH = 1024
MLP_LAST_CHUNKS = 2
MLP_VMEM_LIMIT = 60 * 1024 * 1024

bf16 = jnp.bfloat16
f32 = jnp.float32


def _rms(x, g):
    return x * lax.rsqrt(jnp.mean(x * x, axis=-1, keepdims=True) + EPS) * g


def _dot(a, b):
    return jnp.dot(a, b, preferred_element_type=f32)


def _dot_nt(a, b):
    return lax.dot_general(a, b, (((1,), (1,)), ((), ())), preferred_element_type=f32)


def _dot_tn(a, b):
    return lax.dot_general(a, b, (((0,), (0,)), ((), ())), preferred_element_type=f32)


def _rope128(t, cos, sin):
    lane = lax.broadcasted_iota(jnp.int32, t.shape, 1)
    fwd = pltpu.roll(t, QK_ROPE // 2, axis=1)
    bwd = pltpu.roll(t, LANES - QK_ROPE // 2, axis=1)
    rot = jnp.where((lane & (QK_ROPE // 2)) == 0, -bwd, fwd)
    return t * cos + rot * sin


def _proj_kernel(x_ref, pos_ref, inv_ref, gin_ref, win_ref, gq_ref, wq_ref, gkv_ref, wkv_ref,
                 q_ref, k_ref, v_ref, qs_ref, kvs_ref):
    a = _rms(x_ref[...], gin_ref[...])
    proj = _dot(a.astype(bf16), win_ref[...])

    ang = pos_ref[...].astype(f32) * inv_ref[...]
    cos = jnp.cos(ang)
    sin = jnp.sin(ang)

    qs_ref[...] = (proj[:, _C_QS:_C_QS + SWA_WIDTH] * SWA_SCALE).astype(bf16)
    lane = lax.broadcasted_iota(jnp.int32, (proj.shape[0], LANES), 1)
    lo = lane < SWA_HEAD_DIM
    for idx, col in enumerate((_C_KS, _C_VS)):
        t = proj[:, col:col + LANES]
        tsw = pltpu.roll(t, SWA_HEAD_DIM, axis=1)
        base = idx * 4 * LANES
        kvs_ref[:, base + 0 * LANES:base + 1 * LANES] = jnp.where(lo, t, 0.0).astype(bf16)
        kvs_ref[:, base + 1 * LANES:base + 2 * LANES] = jnp.where(lo, 0.0, tsw).astype(bf16)
        kvs_ref[:, base + 2 * LANES:base + 3 * LANES] = jnp.where(lo, tsw, 0.0).astype(bf16)
        kvs_ref[:, base + 3 * LANES:base + 4 * LANES] = jnp.where(lo, 0.0, t).astype(bf16)

    cq = _rms(proj[:, _C_Q:_C_Q + Q_LORA], gq_ref[...])
    qf = _dot(cq.astype(bf16), wq_ref[...])
    ckv = _rms(proj[:, _C_KV:_C_KV + KV_LORA], gkv_ref[...])
    kvf = _dot(ckv.astype(bf16), wkv_ref[...])
    kr = _rope128(proj[:, _C_KR:_C_KR + LANES], cos, sin).astype(bf16)
    for h in range(MLA_HEADS):
        c0 = h * QK_PAD
        q_ref[:, c0:c0 + LANES] = (qf[:, c0:c0 + LANES] * MLA_SCALE).astype(bf16)
        q_ref[:, c0 + LANES:c0 + QK_PAD] = (
            _rope128(qf[:, c0 + LANES:c0 + QK_PAD], cos, sin) * MLA_SCALE).astype(bf16)
        k_ref[:, c0:c0 + LANES] = kvf[:, c0:c0 + LANES].astype(bf16)
        k_ref[:, c0 + LANES:c0 + QK_PAD] = kr
        v_ref[:, h * V_HEAD:(h + 1) * V_HEAD] = kvf[:, c0 + LANES:c0 + QK_PAD].astype(bf16)


def _mla_kernel(q_ref, k_ref, v_ref, wu_ref, wd_ref, o_ref, wu_o_ref, wd_o_ref, s_ref, vt_ref):
    wu_o_ref[...] = wu_ref[...].astype(bf16)
    wd_o_ref[...] = wd_ref[...].astype(bf16)

    seq = q_ref.shape[0]
    tq = MLA_TQ
    n_tiles = seq // tq

    def scores(i):
        l = (i + 1) * tq
        s_ref[i % 2, 0:l, :] = _dot_nt(k_ref[0:l, :], q_ref[i * tq:(i + 1) * tq, :])

    vt_ref[0:V_HEAD, :] = v_ref[...].T
    vt_ref[V_HEAD:, :] = jnp.ones((vt_ref.shape[0] - V_HEAD, seq), bf16)
    scores(0)
    for i in range(n_tiles):
        if i + 1 < n_tiles:
            scores(i + 1)
        l0 = i * tq
        key = lax.broadcasted_iota(jnp.int32, (tq, tq), 0)
        qry = lax.broadcasted_iota(jnp.int32, (tq, tq), 1)
        sd = jnp.where(key <= qry, s_ref[i % 2, l0:l0 + tq, :], -jnp.inf)
        m = jnp.max(sd, axis=0, keepdims=True)
        if i > 0:
            sa = s_ref[i % 2, 0:l0, :]
            m = jnp.maximum(m, jnp.max(sa, axis=0, keepdims=True))
        acc = _dot(vt_ref[:, l0:l0 + tq], jnp.exp2(sd - m).astype(bf16))
        if i > 0:
            acc = acc + _dot(vt_ref[:, 0:l0], jnp.exp2(sa - m).astype(bf16))
        o_ref[l0:l0 + tq, :] = (acc[0:V_HEAD] / acc[V_HEAD:V_HEAD + 1]).T.astype(o_ref.dtype)


def _swa_bias_table(rel_ref, tab_ref):
    qi = lax.broadcasted_iota(jnp.int32, (BLOCK, 2 * BLOCK), 0) + BLOCK
    kj = lax.broadcasted_iota(jnp.int32, (BLOCK, 2 * BLOCK), 1)
    dist = qi - kj
    in_window = (dist >= 0) & (dist < WINDOW)
    n = jnp.maximum(dist, 0)
    max_exact = REL_BUCKETS // 2
    n_safe = jnp.maximum(n, 1).astype(f32)
    large = max_exact + (jnp.log(n_safe / max_exact) / np.log(REL_MAX_DIST / max_exact)
                         * (REL_BUCKETS - max_exact)).astype(jnp.int32)
    large = jnp.minimum(large, REL_BUCKETS - 1)
    bucket = jnp.where(n < max_exact, n, large)
    for h in range(SWA_Q_HEADS):
        acc = jnp.zeros((BLOCK, 2 * BLOCK), f32)
        for b in range(REL_BUCKETS):
            acc = jnp.where(bucket == b, rel_ref[b, h], acc)
        tab_ref[h] = jnp.where(in_window, acc, -jnp.inf)


def _swa_kernel(rel_ref, sink_ref, q_ref, kvp_ref, kvc_ref, wo_ref, o_ref, wo_o_ref, tab_ref):
    wo_o_ref[...] = wo_ref[...].astype(bf16)
    first = (pl.program_id(0) == 0) & (pl.program_id(1) == 0)

    @pl.when(first)
    def _():
        _swa_bias_table(rel_ref, tab_ref)

    n = pl.program_id(1)
    kj = lax.broadcasted_iota(jnp.int32, (BLOCK, 2 * BLOCK), 1)
    key_ok = kj >= jnp.where(n > 0, 0, BLOCK)
    lane = lax.broadcasted_iota(jnp.int32, (BLOCK, LANES), 1)
    lo = lane < SWA_HEAD_DIM
    band = jnp.concatenate([kvp_ref[...], kvc_ref[...]], axis=0)
    scores = []
    for h in range(SWA_Q_HEADS):
        pair, sub = divmod(h, 2)
        kcol = (2 * (h // SWA_GROUP) + sub) * LANES
        s = _dot_nt(q_ref[:, pair * LANES:(pair + 1) * LANES], band[:, kcol:kcol + LANES])
        scores.append(jnp.where(key_ok, s + tab_ref[h], -jnp.inf))
    probs, rcps = [], []
    for h in range(SWA_Q_HEADS):
        sink = sink_ref[0, h]
        m = jnp.maximum(jnp.max(scores[h], axis=1, keepdims=True), sink)
        e = jnp.exp(scores[h] - m)
        rcps.append(1.0 / (jnp.sum(e, axis=1, keepdims=True) + jnp.exp(sink - m)))
        probs.append(e.astype(bf16))
    for pair in range(SWA_Q_HEADS // 2):
        acc = None
        for sub in range(2):
            h = 2 * pair + sub
            vcol = (4 + 2 * (h // SWA_GROUP) + sub) * LANES
            pv = _dot(probs[h], band[:, vcol:vcol + LANES])
            acc = pv if acc is None else acc + pv
        o_ref[:, pair * LANES:(pair + 1) * LANES] = (
            acc * jnp.where(lo, rcps[2 * pair], rcps[2 * pair + 1])).astype(o_ref.dtype)


def _out_kernel(x_ref, ya_ref, yb_ref, wa_ref, wb_ref, g_ref, h_ref, m_ref):
    h = x_ref[...] + _dot(ya_ref[...], wa_ref[...]) + _dot(yb_ref[...], wb_ref[...])
    h_ref[...] = h
    m_ref[...] = _rms(h, g_ref[...]).astype(m_ref.dtype)


def _mlp_kernel(m_ref, h_hbm, wu_ref, wd_ref, g_ref, o_ref, hbuf, sem):
    i = pl.program_id(0)
    j = pl.program_id(1)
    last_j = pl.num_programs(1) - 1
    tm = o_ref.shape[0]
    h_copy = pltpu.make_async_copy(h_hbm.at[pl.ds(i * tm, tm), :], hbuf, sem)

    def step(first, last):
        n_chunks = MLP_LAST_CHUNKS if last else 1
        rows = tm // n_chunks
        for c in range(n_chunks):
            r = pl.ds(c * rows, rows)
            u = jnp.maximum(_dot(m_ref[r, :], wu_ref[...]), 0.0)
            acc = _dot((u * u).astype(bf16), wd_ref[...])
            if not first:
                acc = o_ref[r, :] + acc
            if last:
                acc = _rms(acc + hbuf[r, :], g_ref[...])
            o_ref[r, :] = acc

    @pl.when(j == 0)
    def _():
        h_copy.start()
        step(first=True, last=False)

    @pl.when((j > 0) & (j < last_j))
    def _():
        step(first=False, last=False)

    @pl.when(j == last_j)
    def _():
        h_copy.wait()
        step(first=False, last=True)


def _params(sem):
    return pltpu.CompilerParams(dimension_semantics=sem, vmem_limit_bytes=VMEM_LIMIT)


def kernel(x, positions, rel_bias, attn_norm, w_in, q_a_norm, w_q_b, kv_a_norm, w_kv_b, sinks,
           w_out, mlp_norm, w_up, w_down, final_norm):
    batch, seq, d = x.shape
    tokens = batch * seq
    nb = seq // BLOCK
    depth = w_in.shape[0]
    assert depth == 1 and d == D_MODEL and seq % MLA_TQ == 0

    wi = w_in[0]
    c_q, c_kv, k_rope, q_s, k_s, v_s = jnp.split(
        wi, (Q_LORA, Q_LORA + KV_LORA, Q_LORA + KV_LORA + QK_ROPE,
             Q_LORA + KV_LORA + QK_ROPE + SWA_WIDTH,
             Q_LORA + KV_LORA + QK_ROPE + SWA_WIDTH + SWA_KV_WIDTH), axis=1)
    win = jnp.concatenate(
        [c_q, c_kv, q_s, k_s, v_s, k_rope, jnp.zeros((d, LANES - QK_ROPE), wi.dtype)],
        axis=1).astype(bf16)
    wq = jnp.pad(w_q_b[0].reshape(Q_LORA, MLA_HEADS, QK_NOPE + QK_ROPE),
                 ((0, 0), (0, 0), (0, QK_PAD - QK_NOPE - QK_ROPE))
                 ).reshape(Q_LORA, MLA_HEADS * QK_PAD).astype(bf16)
    wkv = w_kv_b[0].astype(bf16)

    inv = 1.0 / (ROPE_THETA ** (jnp.arange(0, QK_ROPE, 2, dtype=f32) / QK_ROPE))
    inv128 = jnp.concatenate([inv, inv, jnp.zeros((LANES - QK_ROPE,), f32)])[None, :]

    x2 = x.reshape(tokens, d)
    pos2 = positions.reshape(tokens, 1)
    row = lambda v: v.reshape(1, -1)

    tm = PROJ_TM
    full = lambda shape: pl.BlockSpec(shape, lambda i: (0, 0))
    tok = lambda w: pl.BlockSpec((tm, w), lambda i: (i, 0))
    q, k, v, qs, kvs = pl.pallas_call(
        _proj_kernel,
        grid=(tokens // tm,),
        in_specs=[tok(d), tok(1), full((1, LANES)), full((1, d)), full((d, IN_PAD)),
                  full((1, Q_LORA)), full((Q_LORA, MLA_HEADS * QK_PAD)),
                  full((1, KV_LORA)), full((KV_LORA, MLA_HEADS * (QK_NOPE + V_HEAD)))],
        out_specs=[tok(MLA_HEADS * QK_PAD), tok(MLA_HEADS * QK_PAD), tok(MLA_WIDTH),
                   tok(SWA_WIDTH), tok(8 * LANES)],
        out_shape=[jax.ShapeDtypeStruct((tokens, MLA_HEADS * QK_PAD), bf16),
                   jax.ShapeDtypeStruct((tokens, MLA_HEADS * QK_PAD), bf16),
                   jax.ShapeDtypeStruct((tokens, MLA_WIDTH), bf16),
                   jax.ShapeDtypeStruct((tokens, SWA_WIDTH), bf16),
                   jax.ShapeDtypeStruct((tokens, 8 * LANES), bf16)],
        compiler_params=_params(("arbitrary",)),
        name="proj",
    )(x2, pos2, inv128, row(attn_norm[0]), win, row(q_a_norm[0]), wq, row(kv_a_norm[0]), wkv)

    mla_steps = batch * MLA_HEADS
    wu_rows = d // mla_steps
    wd_rows = MLP_HIDDEN // mla_steps
    slab = lambda rows, cols: pl.BlockSpec((rows, cols), lambda b, h: (b * MLA_HEADS + h, 0))
    y_mla, wu, wd = pl.pallas_call(
        _mla_kernel,
        grid=(batch, MLA_HEADS),
        in_specs=[pl.BlockSpec((seq, QK_PAD), lambda b, h: (b, h)),
                  pl.BlockSpec((seq, QK_PAD), lambda b, h: (b, h)),
                  pl.BlockSpec((seq, V_HEAD), lambda b, h: (b, h)),
                  slab(wu_rows, MLP_HIDDEN), slab(wd_rows, d)],
        out_specs=[pl.BlockSpec((seq, V_HEAD), lambda b, h: (b, h)),
                   slab(wu_rows, MLP_HIDDEN), slab(wd_rows, d)],
        out_shape=[jax.ShapeDtypeStruct((tokens, MLA_WIDTH), bf16),
                   jax.ShapeDtypeStruct((d, MLP_HIDDEN), bf16),
                   jax.ShapeDtypeStruct((MLP_HIDDEN, d), bf16)],
        scratch_shapes=[pltpu.VMEM((2, seq, MLA_TQ), f32), pltpu.VMEM((V_HEAD + 16, seq), bf16)],
        compiler_params=_params(("arbitrary", "arbitrary")),
        name="mla",
    )(q, k, v, w_up[0], w_down[0])

    smem = pl.BlockSpec(memory_space=pltpu.SMEM)
    wo_rows = (MLA_WIDTH + SWA_WIDTH) // (batch * nb)
    wo_slab = pl.BlockSpec((wo_rows, d), lambda b, n: (b * nb + n, 0))
    y_swa, wo = pl.pallas_call(
        _swa_kernel,
        grid=(batch, nb),
        in_specs=[smem, smem,
                  pl.BlockSpec((BLOCK, SWA_WIDTH), lambda b, n: (b * nb + n, 0)),
                  pl.BlockSpec((BLOCK, 8 * LANES), lambda b, n: (b * nb + jnp.maximum(n - 1, 0), 0)),
                  pl.BlockSpec((BLOCK, 8 * LANES), lambda b, n: (b * nb + n, 0)),
                  wo_slab],
        out_specs=[pl.BlockSpec((BLOCK, SWA_WIDTH), lambda b, n: (b * nb + n, 0)), wo_slab],
        out_shape=[jax.ShapeDtypeStruct((tokens, SWA_WIDTH), bf16),
                   jax.ShapeDtypeStruct((MLA_WIDTH + SWA_WIDTH, d), bf16)],
        scratch_shapes=[pltpu.VMEM((SWA_Q_HEADS, BLOCK, 2 * BLOCK), f32)],
        compiler_params=_params(("arbitrary", "arbitrary")),
        name="swa",
    )(rel_bias, row(sinks[0]), qs, kvs, kvs, w_out[0])

    tm = OUT_TM
    h1, m = pl.pallas_call(
        _out_kernel,
        grid=(tokens // tm,),
        in_specs=[tok(d), tok(MLA_WIDTH), tok(SWA_WIDTH),
                  pl.BlockSpec((MLA_WIDTH, d), lambda i: (0, 0)),
                  pl.BlockSpec((SWA_WIDTH, d), lambda i: (1, 0)), full((1, d))],
        out_specs=[tok(d), tok(d)],
        out_shape=[jax.ShapeDtypeStruct((tokens, d), f32),
                   jax.ShapeDtypeStruct((tokens, d), bf16)],
        compiler_params=_params(("arbitrary",)),
        name="outproj",
    )(x2, y_mla, y_swa, wo, wo, row(mlp_norm[0]))

    tm, th = MLP_TM, MLP_TH
    assert MLP_HIDDEN // th >= 2
    out = pl.pallas_call(
        _mlp_kernel,
        grid=(tokens // tm, MLP_HIDDEN // th),
        in_specs=[pl.BlockSpec((tm, d), lambda i, j: (i, 0)),
                  pl.BlockSpec(memory_space=pl.ANY),
                  pl.BlockSpec((d, th), lambda i, j: (0, j)),
                  pl.BlockSpec((th, d), lambda i, j: (j, 0)),
                  pl.BlockSpec((1, d), lambda i, j: (0, 0))],
        out_specs=pl.BlockSpec((tm, d), lambda i, j: (i, 0)),
        out_shape=jax.ShapeDtypeStruct((tokens, d), f32),
        scratch_shapes=[pltpu.VMEM((tm, d), f32), pltpu.SemaphoreType.DMA(())],
        compiler_params=pltpu.CompilerParams(
            dimension_semantics=("arbitrary", "arbitrary"), vmem_limit_bytes=MLP_VMEM_LIMIT),
        name="mlp",
    )(m, h1, wu, wd, row(final_norm))

    return out.reshape(batch, seq, d)
```

```python
import jax
import jax.numpy as jnp
import numpy as np
from jax import lax
from jax.experimental import pallas as pl
from jax.experimental.pallas import tpu as pltpu

D_MODEL = 2048
MLA_HEADS = 8
QK_NOPE = 128
QK_ROPE = 64
V_HEAD = 128
Q_LORA = 512
KV_LORA = 512
ROPE_THETA = 10000.0
SWA_Q_HEADS = 16
SWA_KV_HEADS = 2
SWA_GROUP = SWA_Q_HEADS // SWA_KV_HEADS
SWA_HEAD_DIM = 64
WINDOW = 128
BLOCK = 128
REL_BUCKETS = 32
REL_MAX_DIST = 128
MLP_HIDDEN = 4 * D_MODEL
EPS = 1e-6
MLA_WIDTH = MLA_HEADS * V_HEAD
SWA_WIDTH = SWA_Q_HEADS * SWA_HEAD_DIM
SWA_KV_WIDTH = SWA_KV_HEADS * SWA_HEAD_DIM

LANES = 128
QK_PAD = 2 * LANES
LOG2E = float(np.log2(np.e))
MLA_SCALE = (QK_NOPE + QK_ROPE) ** -0.5 * LOG2E
SWA_SCALE = SWA_HEAD_DIM ** -0.5 * LOG2E

_C_Q = 0
_C_KV = _C_Q + Q_LORA
_C_QS = _C_KV + KV_LORA
_C_KS = _C_QS + SWA_WIDTH
_C_VS = _C_KS + SWA_KV_WIDTH
_C_KR = _C_VS + SWA_KV_WIDTH
IN_PAD = _C_KR + LANES

VMEM_LIMIT = 56 * 1024 * 1024

WIN_ROWS = 256
PROJ_TM = 512
MLA_TQ = 256
OUT_TM = 512
MLP_TM = 1024
MLP_TH = 1024
MLP_LAST_CHUNKS = 2
MLP_VMEM_LIMIT = 60 * 1024 * 1024

bf16 = jnp.bfloat16
f32 = jnp.float32


def _rms(x, g):
    return x * lax.rsqrt(jnp.mean(x * x, axis=-1, keepdims=True) + EPS) * g


def _dot(a, b):
    return jnp.dot(a, b, preferred_element_type=f32)


def _dot_nt(a, b):
    return lax.dot_general(a, b, (((1,), (1,)), ((), ())), preferred_element_type=f32)


def _dot_tn(a, b):
    return lax.dot_general(a, b, (((0,), (0,)), ((), ())), preferred_element_type=f32)


def _rope128(t, cos, sin):
    lane = lax.broadcasted_iota(jnp.int32, t.shape, 1)
    fwd = pltpu.roll(t, QK_ROPE // 2, axis=1)
    bwd = pltpu.roll(t, LANES - QK_ROPE // 2, axis=1)
    rot = jnp.where((lane & (QK_ROPE // 2)) == 0, -bwd, fwd)
    return t * cos + rot * sin


_SRC_KR = Q_LORA + KV_LORA
_SRC_QS = _SRC_KR + QK_ROPE
_SRC_KS = _SRC_QS + SWA_WIDTH
_SRC_END = _SRC_KS + 2 * SWA_KV_WIDTH


def _win_kernel(w_ref, wq_ref, wkv_ref, o_ref, wq_o_ref, wkv_o_ref):
    wkv_o_ref[...] = wkv_ref[...].astype(bf16)
    hd = QK_NOPE + QK_ROPE
    for h in range(MLA_HEADS):
        wq_o_ref[:, h * QK_PAD:h * QK_PAD + hd] = wq_ref[:, h * hd:(h + 1) * hd].astype(bf16)
        wq_o_ref[:, h * QK_PAD + hd:(h + 1) * QK_PAD] = jnp.zeros(
            (wq_o_ref.shape[0], QK_PAD - hd), bf16)
    o_ref[:, _C_Q:_C_QS] = w_ref[:, 0:_SRC_KR].astype(bf16)
    o_ref[:, _C_QS:_C_KS] = w_ref[:, _SRC_QS:_SRC_KS].astype(bf16)
    o_ref[:, _C_KS:_C_KR] = w_ref[:, _SRC_KS:_SRC_END].astype(bf16)
    o_ref[:, _C_KR:_C_KR + QK_ROPE] = w_ref[:, _SRC_KR:_SRC_QS].astype(bf16)
    o_ref[:, _C_KR + QK_ROPE:IN_PAD] = jnp.zeros(
        (o_ref.shape[0], IN_PAD - _C_KR - QK_ROPE), bf16)


def _proj_kernel(x_ref, pos_ref, inv_ref, gin_ref, win_ref, gq_ref, wq_ref, gkv_ref, wkv_ref,
                 q_ref, k_ref, v_ref, qs_ref, ks_ref, vst_ref):
    a = _rms(x_ref[...], gin_ref[...])
    proj = _dot(a.astype(bf16), win_ref[...])

    ang = pos_ref[...].astype(f32) * inv_ref[...]
    cos = jnp.cos(ang)
    sin = jnp.sin(ang)

    tm = proj.shape[0]
    qs_ref[...] = (proj[:, _C_QS:_C_QS + SWA_WIDTH] * SWA_SCALE).astype(bf16)
    lo = lax.broadcasted_iota(jnp.int32, (tm, LANES), 1) < SWA_HEAD_DIM
    t = proj[:, _C_KS:_C_KS + LANES]
    tsw = pltpu.roll(t, SWA_HEAD_DIM, axis=1)
    ks_ref[:, 0 * LANES:1 * LANES] = jnp.where(lo, t, 0.0).astype(bf16)
    ks_ref[:, 1 * LANES:2 * LANES] = jnp.where(lo, 0.0, tsw).astype(bf16)
    ks_ref[:, 2 * LANES:3 * LANES] = jnp.where(lo, tsw, 0.0).astype(bf16)
    ks_ref[:, 3 * LANES:4 * LANES] = jnp.where(lo, 0.0, t).astype(bf16)
    vt = proj[:, _C_VS:_C_VS + LANES].T.astype(bf16)
    ones = jnp.ones((SWA_HEAD_DIM, tm), bf16)
    for kvh in range(SWA_KV_HEADS):
        v_t = vt[kvh * SWA_HEAD_DIM:(kvh + 1) * SWA_HEAD_DIM]
        vst_ref[(2 * kvh) * LANES:(2 * kvh + 1) * LANES, :] = jnp.concatenate([v_t, ones], axis=0)
        vst_ref[(2 * kvh + 1) * LANES:(2 * kvh + 2) * LANES, :] = jnp.concatenate([ones, v_t], axis=0)

    cq = _rms(proj[:, _C_Q:_C_Q + Q_LORA], gq_ref[...])
    qf = _dot(cq.astype(bf16), wq_ref[...])
    ckv = _rms(proj[:, _C_KV:_C_KV + KV_LORA], gkv_ref[...])
    kvf = _dot(ckv.astype(bf16), wkv_ref[...])
    kr = _rope128(proj[:, _C_KR:_C_KR + LANES], cos, sin).astype(bf16)
    for h in range(MLA_HEADS):
        c0 = h * QK_PAD
        q_ref[:, c0:c0 + LANES] = (qf[:, c0:c0 + LANES] * MLA_SCALE).astype(bf16)
        q_ref[:, c0 + LANES:c0 + QK_PAD] = (
            _rope128(qf[:, c0 + LANES:c0 + QK_PAD], cos, sin) * MLA_SCALE).astype(bf16)
        k_ref[:, c0:c0 + LANES] = kvf[:, c0:c0 + LANES].astype(bf16)
        k_ref[:, c0 + LANES:c0 + QK_PAD] = kr
        v_ref[:, h * V_HEAD:(h + 1) * V_HEAD] = kvf[:, c0 + LANES:c0 + QK_PAD].astype(bf16)


def _mla_kernel(q_ref, k_ref, v_ref, wu_ref, wd_ref, o_ref, wu_o_ref, wd_o_ref, s_ref, vt_ref):
    wu_o_ref[...] = wu_ref[...].astype(bf16)
    wd_o_ref[...] = wd_ref[...].astype(bf16)

    seq = q_ref.shape[0]
    tq = MLA_TQ
    n_tiles = seq // tq

    def scores(i):
        l = (i + 1) * tq
        s_ref[i % 2, 0:l, :] = _dot_nt(k_ref[0:l, :], q_ref[i * tq:(i + 1) * tq, :])

    vt_ref[0:V_HEAD, :] = v_ref[...].T
    vt_ref[V_HEAD:, :] = jnp.ones((vt_ref.shape[0] - V_HEAD, seq), bf16)
    scores(0)
    for i in range(n_tiles):
        if i + 1 < n_tiles:
            scores(i + 1)
        l0 = i * tq
        key = lax.broadcasted_iota(jnp.int32, (tq, tq), 0)
        qry = lax.broadcasted_iota(jnp.int32, (tq, tq), 1)
        sd = jnp.where(key <= qry, s_ref[i % 2, l0:l0 + tq, :], -jnp.inf)
        m = jnp.max(sd, axis=0, keepdims=True)
        if i > 0:
            sa = s_ref[i % 2, 0:l0, :]
            m = jnp.maximum(m, jnp.max(sa, axis=0, keepdims=True))
        acc = _dot(vt_ref[:, l0:l0 + tq], jnp.exp2(sd - m).astype(bf16))
        if i > 0:
            acc = acc + _dot(vt_ref[:, 0:l0], jnp.exp2(sa - m).astype(bf16))
        o_ref[l0:l0 + tq, :] = (acc[0:V_HEAD] / acc[V_HEAD:V_HEAD + 1]).T.astype(o_ref.dtype)


def _swa_bias_table(rel_ref, tab_ref):
    kj = lax.broadcasted_iota(jnp.int32, (2 * BLOCK, BLOCK), 0)
    qi = lax.broadcasted_iota(jnp.int32, (2 * BLOCK, BLOCK), 1) + BLOCK
    dist = qi - kj
    in_window = (dist >= 0) & (dist < WINDOW)
    n = jnp.maximum(dist, 0)
    max_exact = REL_BUCKETS // 2
    n_safe = jnp.maximum(n, 1).astype(f32)
    large = max_exact + (jnp.log(n_safe / max_exact) / np.log(REL_MAX_DIST / max_exact)
                         * (REL_BUCKETS - max_exact)).astype(jnp.int32)
    large = jnp.minimum(large, REL_BUCKETS - 1)
    bucket = jnp.where(n < max_exact, n, large)
    for h in range(SWA_Q_HEADS):
        acc = jnp.zeros((2 * BLOCK, BLOCK), f32)
        for b in range(REL_BUCKETS):
            acc = jnp.where(bucket == b, rel_ref[b, h] * LOG2E, acc)
        tab_ref[0, h] = jnp.where(in_window, acc, -jnp.inf)
        tab_ref[1, h] = jnp.where(in_window & (kj >= BLOCK), acc, -jnp.inf)


def _swa_kernel(rel_ref, sink_ref, q_ref, kp_ref, kc_ref, vp_ref, vc_ref, wo_ref,
                o_ref, wo_o_ref, tab_ref):
    wo_o_ref[...] = wo_ref[...].astype(bf16)
    first = (pl.program_id(0) == 0) & (pl.program_id(1) == 0)

    @pl.when(first)
    def _():
        _swa_bias_table(rel_ref, tab_ref)

    variant = (pl.program_id(1) == 0).astype(jnp.int32)
    kband = jnp.concatenate([kp_ref[...], kc_ref[...]], axis=0)
    vband = jnp.concatenate([vp_ref[...], vc_ref[...]], axis=1)
    hd = SWA_HEAD_DIM
    for pair in range(SWA_Q_HEADS // 2):
        q = q_ref[:, pair * LANES:(pair + 1) * LANES]
        halves = []
        for sub in range(2):
            h = 2 * pair + sub
            form = 2 * (h // SWA_GROUP) + sub
            s = _dot_nt(kband[:, form * LANES:(form + 1) * LANES], q) + tab_ref[variant, h]
            sink = sink_ref[0, h] * LOG2E
            m = jnp.maximum(jnp.max(s, axis=0, keepdims=True), sink)
            r = _dot(vband[form * LANES:(form + 1) * LANES, :], jnp.exp2(s - m).astype(bf16))
            out_t, den = (r[0:hd], r[hd:hd + 1]) if sub == 0 else (r[hd:], r[0:1])
            halves.append(out_t / (den + jnp.exp2(sink - m)))
        o_ref[:, pair * LANES:(pair + 1) * LANES] = (
            jnp.concatenate(halves, axis=0).T.astype(o_ref.dtype))


def _out_kernel(x_ref, ya_ref, yb_ref, wa_ref, wb_ref, g_ref, h_ref, m_ref):
    h = x_ref[...] + _dot(ya_ref[...], wa_ref[...]) + _dot(yb_ref[...], wb_ref[...])
    h_ref[...] = h
    m_ref[...] = _rms(h, g_ref[...]).astype(m_ref.dtype)


def _mlp_kernel(m_ref, h_hbm, wu_ref, wd_ref, g_ref, o_ref, hbuf, sem):
    i = pl.program_id(0)
    j = pl.program_id(1)
    last_j = pl.num_programs(1) - 1
    tm = o_ref.shape[0]
    h_copy = pltpu.make_async_copy(h_hbm.at[pl.ds(i * tm, tm), :], hbuf, sem)

    def step(first, last):
        n_chunks = MLP_LAST_CHUNKS if last else 1
        rows = tm // n_chunks
        for c in range(n_chunks):
            r = pl.ds(c * rows, rows)
            u = jnp.maximum(_dot(m_ref[r, :], wu_ref[...]), 0.0)
            acc = _dot((u * u).astype(bf16), wd_ref[...])
            if not first:
                acc = o_ref[r, :] + acc
            if last:
                acc = _rms(acc + hbuf[r, :], g_ref[...])
            o_ref[r, :] = acc

    @pl.when(j == 0)
    def _():
        h_copy.start()
        step(first=True, last=False)

    @pl.when((j > 0) & (j < last_j))
    def _():
        step(first=False, last=False)

    @pl.when(j == last_j)
    def _():
        h_copy.wait()
        step(first=False, last=True)


def _params(sem):
    return pltpu.CompilerParams(dimension_semantics=sem, vmem_limit_bytes=VMEM_LIMIT)


def kernel(x, positions, rel_bias, attn_norm, w_in, q_a_norm, w_q_b, kv_a_norm, w_kv_b, sinks,
           w_out, mlp_norm, w_up, w_down, final_norm):
    batch, seq, d = x.shape
    tokens = batch * seq
    nb = seq // BLOCK
    depth = w_in.shape[0]
    assert depth == 1 and d == D_MODEL and seq % MLA_TQ == 0

    assert w_in.shape[2] == _SRC_END
    n_prep = d // WIN_ROWS
    lat_rows = Q_LORA // n_prep
    assert KV_LORA == Q_LORA
    slab3 = lambda rows, cols: pl.BlockSpec((None, rows, cols), lambda i: (0, i, 0))
    slab2 = lambda rows, cols: pl.BlockSpec((rows, cols), lambda i: (i, 0))
    kv_cols = MLA_HEADS * (QK_NOPE + V_HEAD)
    win, wq, wkv = pl.pallas_call(
        _win_kernel,
        grid=(n_prep,),
        in_specs=[slab3(WIN_ROWS, _SRC_END),
                  slab3(lat_rows, MLA_HEADS * (QK_NOPE + QK_ROPE)), slab3(lat_rows, kv_cols)],
        out_specs=[slab2(WIN_ROWS, IN_PAD), slab2(lat_rows, MLA_HEADS * QK_PAD),
                   slab2(lat_rows, kv_cols)],
        out_shape=[jax.ShapeDtypeStruct((d, IN_PAD), bf16),
                   jax.ShapeDtypeStruct((Q_LORA, MLA_HEADS * QK_PAD), bf16),
                   jax.ShapeDtypeStruct((KV_LORA, kv_cols), bf16)],
        compiler_params=_params(("arbitrary",)),
        name="winprep",
    )(w_in, w_q_b, w_kv_b)

    inv = 1.0 / (ROPE_THETA ** (jnp.arange(0, QK_ROPE, 2, dtype=f32) / QK_ROPE))
    inv128 = jnp.concatenate([inv, inv, jnp.zeros((LANES - QK_ROPE,), f32)])[None, :]

    x2 = x.reshape(tokens, d)
    pos2 = positions.reshape(tokens, 1)
    row = lambda v: v.reshape(1, -1)

    tm = PROJ_TM
    full = lambda shape: pl.BlockSpec(shape, lambda i: (0, 0))
    tok = lambda w: pl.BlockSpec((tm, w), lambda i: (i, 0))
    n_forms = 2 * SWA_KV_HEADS
    q, k, v, qs, ks, vst = pl.pallas_call(
        _proj_kernel,
        grid=(tokens // tm,),
        in_specs=[tok(d), tok(1), full((1, LANES)), full((1, d)), full((d, IN_PAD)),
                  full((1, Q_LORA)), full((Q_LORA, MLA_HEADS * QK_PAD)),
                  full((1, KV_LORA)), full((KV_LORA, MLA_HEADS * (QK_NOPE + V_HEAD)))],
        out_specs=[tok(MLA_HEADS * QK_PAD), tok(MLA_HEADS * QK_PAD), tok(MLA_WIDTH),
                   tok(SWA_WIDTH), tok(n_forms * LANES),
                   pl.BlockSpec((n_forms * LANES, tm), lambda i: (0, i))],
        out_shape=[jax.ShapeDtypeStruct((tokens, MLA_HEADS * QK_PAD), bf16),
                   jax.ShapeDtypeStruct((tokens, MLA_HEADS * QK_PAD), bf16),
                   jax.ShapeDtypeStruct((tokens, MLA_WIDTH), bf16),
                   jax.ShapeDtypeStruct((tokens, SWA_WIDTH), bf16),
                   jax.ShapeDtypeStruct((tokens, n_forms * LANES), bf16),
                   jax.ShapeDtypeStruct((n_forms * LANES, tokens), bf16)],
        compiler_params=_params(("arbitrary",)),
        name="proj",
    )(x2, pos2, inv128, row(attn_norm[0]), win, row(q_a_norm[0]), wq, row(kv_a_norm[0]), wkv)

    mla_steps = batch * MLA_HEADS
    wu_rows = d // mla_steps
    wd_rows = MLP_HIDDEN // mla_steps
    slab = lambda rows, cols: pl.BlockSpec((rows, cols), lambda b, h: (b * MLA_HEADS + h, 0))
    y_mla, wu, wd = pl.pallas_call(
        _mla_kernel,
        grid=(batch, MLA_HEADS),
        in_specs=[pl.BlockSpec((seq, QK_PAD), lambda b, h: (b, h)),
                  pl.BlockSpec((seq, QK_PAD), lambda b, h: (b, h)),
                  pl.BlockSpec((seq, V_HEAD), lambda b, h: (b, h)),
                  slab(wu_rows, MLP_HIDDEN), slab(wd_rows, d)],
        out_specs=[pl.BlockSpec((seq, V_HEAD), lambda b, h: (b, h)),
                   slab(wu_rows, MLP_HIDDEN), slab(wd_rows, d)],
        out_shape=[jax.ShapeDtypeStruct((tokens, MLA_WIDTH), bf16),
                   jax.ShapeDtypeStruct((d, MLP_HIDDEN), bf16),
                   jax.ShapeDtypeStruct((MLP_HIDDEN, d), bf16)],
        scratch_shapes=[pltpu.VMEM((2, seq, MLA_TQ), f32), pltpu.VMEM((V_HEAD + 16, seq), bf16)],
        compiler_params=_params(("arbitrary", "arbitrary")),
        name="mla",
    )(q, k, v, w_up[0], w_down[0])

    smem = pl.BlockSpec(memory_space=pltpu.SMEM)
    wo_rows = (MLA_WIDTH + SWA_WIDTH) // (batch * nb)
    wo_slab = pl.BlockSpec((wo_rows, d), lambda b, n: (b * nb + n, 0))
    y_swa, wo = pl.pallas_call(
        _swa_kernel,
        grid=(batch, nb),
        in_specs=[smem, smem,
                  pl.BlockSpec((BLOCK, SWA_WIDTH), lambda b, n: (b * nb + n, 0)),
                  pl.BlockSpec((BLOCK, n_forms * LANES),
                               lambda b, n: (b * nb + jnp.maximum(n - 1, 0), 0)),
                  pl.BlockSpec((BLOCK, n_forms * LANES), lambda b, n: (b * nb + n, 0)),
                  pl.BlockSpec((n_forms * LANES, BLOCK),
                               lambda b, n: (0, b * nb + jnp.maximum(n - 1, 0))),
                  pl.BlockSpec((n_forms * LANES, BLOCK), lambda b, n: (0, b * nb + n)),
                  wo_slab],
        out_specs=[pl.BlockSpec((BLOCK, SWA_WIDTH), lambda b, n: (b * nb + n, 0)), wo_slab],
        out_shape=[jax.ShapeDtypeStruct((tokens, SWA_WIDTH), bf16),
                   jax.ShapeDtypeStruct((MLA_WIDTH + SWA_WIDTH, d), bf16)],
        scratch_shapes=[pltpu.VMEM((2, SWA_Q_HEADS, 2 * BLOCK, BLOCK), f32)],
        compiler_params=_params(("arbitrary", "arbitrary")),
        name="swa",
    )(rel_bias, row(sinks[0]), qs, ks, ks, vst, vst, w_out[0])

    tm = OUT_TM
    h1, m = pl.pallas_call(
        _out_kernel,
        grid=(tokens // tm,),
        in_specs=[tok(d), tok(MLA_WIDTH), tok(SWA_WIDTH),
                  pl.BlockSpec((MLA_WIDTH, d), lambda i: (0, 0)),
                  pl.BlockSpec((SWA_WIDTH, d), lambda i: (1, 0)), full((1, d))],
        out_specs=[tok(d), tok(d)],
        out_shape=[jax.ShapeDtypeStruct((tokens, d), f32),
                   jax.ShapeDtypeStruct((tokens, d), bf16)],
        compiler_params=_params(("arbitrary",)),
        name="outproj",
    )(x2, y_mla, y_swa, wo, wo, row(mlp_norm[0]))

    tm, th = MLP_TM, MLP_TH
    assert MLP_HIDDEN // th >= 2
    out = pl.pallas_call(
        _mlp_kernel,
        grid=(tokens // tm, MLP_HIDDEN // th),
        in_specs=[pl.BlockSpec((tm, d), lambda i, j: (i, 0)),
                  pl.BlockSpec(memory_space=pl.ANY),
                  pl.BlockSpec((d, th), lambda i, j: (0, j)),
                  pl.BlockSpec((th, d), lambda i, j: (j, 0)),
                  pl.BlockSpec((1, d), lambda i, j: (0, 0))],
        out_specs=pl.BlockSpec((tm, d), lambda i, j: (i, 0)),
        out_shape=jax.ShapeDtypeStruct((tokens, d), f32),
        scratch_shapes=[pltpu.VMEM((tm, d), f32), pltpu.SemaphoreType.DMA(())],
        compiler_params=pltpu.CompilerParams(
            dimension_semantics=("arbitrary", "arbitrary"), vmem_limit_bytes=MLP_VMEM_LIMIT),
        name="mlp",
    )(m, h1, wu, wd, row(final_norm))

    return out.reshape(batch, seq, d)
```

```python
import jax
import jax.numpy as jnp
import numpy as np
from jax import lax
from jax.experimental import pallas as pl
from jax.experimental.pallas import tpu as pltpu

D_MODEL = 2048
MLA_HEADS = 8
QK_NOPE = 128
QK_ROPE = 64
V_HEAD = 128
Q_LORA = 512
KV_LORA = 512
ROPE_THETA = 10000.0
SWA_Q_HEADS = 16
SWA_KV_HEADS = 2
SWA_GROUP = SWA_Q_HEADS // SWA_KV_HEADS
SWA_HEAD_DIM = 64
WINDOW = 128
BLOCK = 128
REL_BUCKETS = 32
REL_MAX_DIST = 128
MLP_HIDDEN = 4 * D_MODEL
EPS = 1e-6
MLA_WIDTH = MLA_HEADS * V_HEAD
SWA_WIDTH = SWA_Q_HEADS * SWA_HEAD_DIM
SWA_KV_WIDTH = SWA_KV_HEADS * SWA_HEAD_DIM

LANES = 128
QK_PAD = 2 * LANES
LOG2E = float(np.log2(np.e))
MLA_SCALE = (QK_NOPE + QK_ROPE) ** -0.5 * LOG2E
SWA_SCALE = SWA_HEAD_DIM ** -0.5 * LOG2E

_C_Q = 0
_C_KV = _C_Q + Q_LORA
_C_QS = _C_KV + KV_LORA
_C_KS = _C_QS + SWA_WIDTH
_C_VS = _C_KS + SWA_KV_WIDTH
_C_KR = _C_VS + SWA_KV_WIDTH
IN_PAD = _C_KR + LANES

VMEM_LIMIT = 56 * 1024 * 1024

WIN_ROWS = 256
PROJ_TM = 512
MLA_TQ = 256
MLA_HEADS_PER_STEP = 2
OUT_TM = 512
MLP_TM = 1024
MLP_TH = 1024
MLP_LAST_CHUNKS = 2
MLP_VMEM_LIMIT = 60 * 1024 * 1024

bf16 = jnp.bfloat16
f32 = jnp.float32


def _rms(x, g):
    return x * lax.rsqrt(jnp.mean(x * x, axis=-1, keepdims=True) + EPS) * g


def _dot(a, b):
    return jnp.dot(a, b, preferred_element_type=f32)


def _dot_nt(a, b):
    return lax.dot_general(a, b, (((1,), (1,)), ((), ())), preferred_element_type=f32)


def _dot_tn(a, b):
    return lax.dot_general(a, b, (((0,), (0,)), ((), ())), preferred_element_type=f32)


def _rope128(t, cos, sin):
    lane = lax.broadcasted_iota(jnp.int32, t.shape, 1)
    fwd = pltpu.roll(t, QK_ROPE // 2, axis=1)
    bwd = pltpu.roll(t, LANES - QK_ROPE // 2, axis=1)
    rot = jnp.where((lane & (QK_ROPE // 2)) == 0, -bwd, fwd)
    return t * cos + rot * sin


_SRC_KR = Q_LORA + KV_LORA
_SRC_QS = _SRC_KR + QK_ROPE
_SRC_KS = _SRC_QS + SWA_WIDTH
_SRC_END = _SRC_KS + 2 * SWA_KV_WIDTH


def _win_kernel(w_ref, wq_ref, wkv_ref, o_ref, wq_o_ref, wkv_o_ref):
    wkv_o_ref[...] = wkv_ref[...].astype(bf16)
    hd = QK_NOPE + QK_ROPE
    for h in range(MLA_HEADS):
        wq_o_ref[:, h * QK_PAD:h * QK_PAD + hd] = wq_ref[:, h * hd:(h + 1) * hd].astype(bf16)
        wq_o_ref[:, h * QK_PAD + hd:(h + 1) * QK_PAD] = jnp.zeros(
            (wq_o_ref.shape[0], QK_PAD - hd), bf16)
    o_ref[_C_Q:_C_QS, :] = w_ref[0:_SRC_KR, :].astype(bf16)
    o_ref[_C_QS:_C_KS, :] = w_ref[_SRC_QS:_SRC_KS, :].astype(bf16)
    o_ref[_C_KS:_C_KR, :] = w_ref[_SRC_KS:_SRC_END, :].astype(bf16)
    o_ref[_C_KR:_C_KR + QK_ROPE, :] = w_ref[_SRC_KR:_SRC_QS, :].astype(bf16)
    o_ref[_C_KR + QK_ROPE:IN_PAD, :] = jnp.zeros(
        (IN_PAD - _C_KR - QK_ROPE, o_ref.shape[1]), bf16)


def _proj_kernel(x_ref, pos_ref, inv_ref, gin_ref, win_ref, gq_ref, wq_ref, gkv_ref, wkv_ref,
                 q_ref, k_ref, v_ref, qs_ref, ks_ref, vst_ref):
    a = _rms(x_ref[...], gin_ref[...])
    proj = _dot_nt(a.astype(bf16), win_ref[...])

    ang = pos_ref[...].astype(f32) * inv_ref[...]
    cos = jnp.cos(ang)
    sin = jnp.sin(ang)

    tm = proj.shape[0]
    qs_ref[...] = (proj[:, _C_QS:_C_QS + SWA_WIDTH] * SWA_SCALE).astype(bf16)
    lo = lax.broadcasted_iota(jnp.int32, (tm, LANES), 1) < SWA_HEAD_DIM
    t = proj[:, _C_KS:_C_KS + LANES]
    tsw = pltpu.roll(t, SWA_HEAD_DIM, axis=1)
    ks_ref[:, 0 * LANES:1 * LANES] = jnp.where(lo, t, 0.0).astype(bf16)
    ks_ref[:, 1 * LANES:2 * LANES] = jnp.where(lo, 0.0, tsw).astype(bf16)
    ks_ref[:, 2 * LANES:3 * LANES] = jnp.where(lo, tsw, 0.0).astype(bf16)
    ks_ref[:, 3 * LANES:4 * LANES] = jnp.where(lo, 0.0, t).astype(bf16)
    vt = proj[:, _C_VS:_C_VS + LANES].T.astype(bf16)
    ones = jnp.ones((SWA_HEAD_DIM, tm), bf16)
    for kvh in range(SWA_KV_HEADS):
        v_t = vt[kvh * SWA_HEAD_DIM:(kvh + 1) * SWA_HEAD_DIM]
        vst_ref[(2 * kvh) * LANES:(2 * kvh + 1) * LANES, :] = jnp.concatenate([v_t, ones], axis=0)
        vst_ref[(2 * kvh + 1) * LANES:(2 * kvh + 2) * LANES, :] = jnp.concatenate([ones, v_t], axis=0)

    cq = _rms(proj[:, _C_Q:_C_Q + Q_LORA], gq_ref[...])
    qf = _dot(cq.astype(bf16), wq_ref[...])
    ckv = _rms(proj[:, _C_KV:_C_KV + KV_LORA], gkv_ref[...])
    kvf = _dot(ckv.astype(bf16), wkv_ref[...])
    kr = _rope128(proj[:, _C_KR:_C_KR + LANES], cos, sin).astype(bf16)
    for h in range(MLA_HEADS):
        c0 = h * QK_PAD
        q_ref[:, c0:c0 + LANES] = (qf[:, c0:c0 + LANES] * MLA_SCALE).astype(bf16)
        q_ref[:, c0 + LANES:c0 + QK_PAD] = (
            _rope128(qf[:, c0 + LANES:c0 + QK_PAD], cos, sin) * MLA_SCALE).astype(bf16)
        k_ref[:, c0:c0 + LANES] = kvf[:, c0:c0 + LANES].astype(bf16)
        k_ref[:, c0 + LANES:c0 + QK_PAD] = kr
        v_ref[:, h * V_HEAD:(h + 1) * V_HEAD] = kvf[:, c0 + LANES:c0 + QK_PAD].astype(bf16)


def _mla_kernel(q_ref, k_ref, v_ref, wu_ref, wd_ref, o_ref, wu_o_ref, wd_o_ref, s_ref, vt_ref):
    wu_o_ref[...] = wu_ref[...].astype(bf16)
    wd_o_ref[...] = wd_ref[...].astype(bf16)

    seq = q_ref.shape[0]
    tq = MLA_TQ
    n_tiles = seq // tq
    heads = range(MLA_HEADS_PER_STEP)
    order = list(reversed(range(n_tiles)))

    def scores(g, i):
        l = (i + 1) * tq
        s_ref[g, i % 2, 0:l, :] = _dot_nt(k_ref[0:l, g * QK_PAD:(g + 1) * QK_PAD],
                                          q_ref[i * tq:(i + 1) * tq, g * QK_PAD:(g + 1) * QK_PAD])

    for g in heads:
        vt_ref[g, 0:V_HEAD, :] = v_ref[:, g * V_HEAD:(g + 1) * V_HEAD].T
        vt_ref[g, V_HEAD:, :] = jnp.ones((vt_ref.shape[1] - V_HEAD, seq), bf16)
        scores(g, order[0])
    for step, i in enumerate(order):
        for g in heads:
            if step + 1 < n_tiles:
                scores(g, order[step + 1])
            l0 = i * tq
            key = lax.broadcasted_iota(jnp.int32, (tq, tq), 0)
            qry = lax.broadcasted_iota(jnp.int32, (tq, tq), 1)
            sd = jnp.where(key <= qry, s_ref[g, i % 2, l0:l0 + tq, :], -jnp.inf)
            m = jnp.max(sd, axis=0, keepdims=True)
            if i > 0:
                sa = s_ref[g, i % 2, 0:l0, :]
                m = jnp.maximum(m, jnp.max(sa, axis=0, keepdims=True))
            acc = _dot(vt_ref[g, :, l0:l0 + tq], jnp.exp2(sd - m).astype(bf16))
            if i > 0:
                acc = acc + _dot(vt_ref[g, :, 0:l0], jnp.exp2(sa - m).astype(bf16))
            o_ref[l0:l0 + tq, g * V_HEAD:(g + 1) * V_HEAD] = (
                acc[0:V_HEAD] / acc[V_HEAD:V_HEAD + 1]).T.astype(o_ref.dtype)


def _swa_bias_table(rel_ref, tab_ref):
    kj = lax.broadcasted_iota(jnp.int32, (2 * BLOCK, BLOCK), 0)
    qi = lax.broadcasted_iota(jnp.int32, (2 * BLOCK, BLOCK), 1) + BLOCK
    dist = qi - kj
    in_window = (dist >= 0) & (dist < WINDOW)
    n = jnp.maximum(dist, 0)
    max_exact = REL_BUCKETS // 2
    n_safe = jnp.maximum(n, 1).astype(f32)
    large = max_exact + (jnp.log(n_safe / max_exact) / np.log(REL_MAX_DIST / max_exact)
                         * (REL_BUCKETS - max_exact)).astype(jnp.int32)
    large = jnp.minimum(large, REL_BUCKETS - 1)
    bucket = jnp.where(n < max_exact, n, large)
    for h in range(SWA_Q_HEADS):
        acc = jnp.zeros((2 * BLOCK, BLOCK), f32)
        for b in range(REL_BUCKETS):
            acc = jnp.where(bucket == b, rel_ref[b, h] * LOG2E, acc)
        tab_ref[0, h] = jnp.where(in_window, acc, -jnp.inf)
        tab_ref[1, h] = jnp.where(in_window & (kj >= BLOCK), acc, -jnp.inf)


def _swa_kernel(rel_ref, sink_ref, q_ref, kp_ref, kc_ref, vp_ref, vc_ref, wo_ref,
                o_ref, wo_o_ref, tab_ref):
    wo_o_ref[...] = wo_ref[...].astype(bf16)
    first = (pl.program_id(0) == 0) & (pl.program_id(1) == 0)

    @pl.when(first)
    def _():
        _swa_bias_table(rel_ref, tab_ref)

    variant = (pl.program_id(1) == 0).astype(jnp.int32)
    kband = jnp.concatenate([kp_ref[...], kc_ref[...]], axis=0)
    vband = jnp.concatenate([vp_ref[...], vc_ref[...]], axis=1)
    hd = SWA_HEAD_DIM
    for pair in range(SWA_Q_HEADS // 2):
        q = q_ref[:, pair * LANES:(pair + 1) * LANES]
        halves = []
        for sub in range(2):
            h = 2 * pair + sub
            form = 2 * (h // SWA_GROUP) + sub
            s = _dot_nt(kband[:, form * LANES:(form + 1) * LANES], q) + tab_ref[variant, h]
            sink = sink_ref[0, h] * LOG2E
            m = jnp.maximum(jnp.max(s, axis=0, keepdims=True), sink)
            r = _dot(vband[form * LANES:(form + 1) * LANES, :], jnp.exp2(s - m).astype(bf16))
            out_t, den = (r[0:hd], r[hd:hd + 1]) if sub == 0 else (r[hd:], r[0:1])
            halves.append(out_t / (den + jnp.exp2(sink - m)))
        o_ref[:, pair * LANES:(pair + 1) * LANES] = (
            jnp.concatenate(halves, axis=0).T.astype(o_ref.dtype))


def _out_kernel(x_ref, ya_ref, yb_ref, wa_ref, wb_ref, g_ref, h_ref, m_ref):
    h = x_ref[...] + _dot(ya_ref[...], wa_ref[...]) + _dot(yb_ref[...], wb_ref[...])
    h_ref[...] = h
    m_ref[...] = _rms(h, g_ref[...]).astype(m_ref.dtype)


def _mlp_kernel(m_ref, h_hbm, wu_ref, wd_ref, g_ref, o_ref, hbuf, sem):
    i = pl.program_id(0)
    j = pl.program_id(1)
    last_j = pl.num_programs(1) - 1
    tm = o_ref.shape[0]
    h_copy = pltpu.make_async_copy(h_hbm.at[pl.ds(i * tm, tm), :], hbuf, sem)

    def step(first, last):
        n_chunks = MLP_LAST_CHUNKS if last else 1
        rows = tm // n_chunks
        for c in range(n_chunks):
            r = pl.ds(c * rows, rows)
            u = jnp.maximum(_dot(m_ref[r, :], wu_ref[...]), 0.0)
            acc = _dot((u * u).astype(bf16), wd_ref[...])
            if not first:
                acc = o_ref[r, :] + acc
            if last:
                acc = _rms(acc + hbuf[r, :], g_ref[...])
            o_ref[r, :] = acc

    @pl.when(j == 0)
    def _():
        h_copy.start()
        step(first=True, last=False)

    @pl.when((j > 0) & (j < last_j))
    def _():
        step(first=False, last=False)

    @pl.when(j == last_j)
    def _():
        h_copy.wait()
        step(first=False, last=True)


def _params(sem):
    return pltpu.CompilerParams(dimension_semantics=sem, vmem_limit_bytes=VMEM_LIMIT)


def kernel(x, positions, rel_bias, attn_norm, w_in, q_a_norm, w_q_b, kv_a_norm, w_kv_b, sinks,
           w_out, mlp_norm, w_up, w_down, final_norm):
    batch, seq, d = x.shape
    tokens = batch * seq
    nb = seq // BLOCK
    depth = w_in.shape[0]
    assert depth == 1 and d == D_MODEL and seq % MLA_TQ == 0

    assert w_in.shape[2] == _SRC_END
    n_prep = d // WIN_ROWS
    lat_rows = Q_LORA // n_prep
    assert KV_LORA == Q_LORA
    slab3 = lambda rows, cols: pl.BlockSpec((None, rows, cols), lambda i: (0, i, 0))
    slab2 = lambda rows, cols: pl.BlockSpec((rows, cols), lambda i: (i, 0))
    kv_cols = MLA_HEADS * (QK_NOPE + V_HEAD)
    win, wq, wkv = pl.pallas_call(
        _win_kernel,
        grid=(n_prep,),
        in_specs=[pl.BlockSpec((_SRC_END, WIN_ROWS), lambda i: (0, i)),
                  slab3(lat_rows, MLA_HEADS * (QK_NOPE + QK_ROPE)), slab3(lat_rows, kv_cols)],
        out_specs=[pl.BlockSpec((IN_PAD, WIN_ROWS), lambda i: (0, i)),
                   slab2(lat_rows, MLA_HEADS * QK_PAD), slab2(lat_rows, kv_cols)],
        out_shape=[jax.ShapeDtypeStruct((IN_PAD, d), bf16),
                   jax.ShapeDtypeStruct((Q_LORA, MLA_HEADS * QK_PAD), bf16),
                   jax.ShapeDtypeStruct((KV_LORA, kv_cols), bf16)],
        compiler_params=_params(("arbitrary",)),
        name="winprep",
    )(jnp.transpose(w_in[0]), w_q_b, w_kv_b)

    inv = 1.0 / (ROPE_THETA ** (jnp.arange(0, QK_ROPE, 2, dtype=f32) / QK_ROPE))
    inv128 = jnp.concatenate([inv, inv, jnp.zeros((LANES - QK_ROPE,), f32)])[None, :]

    x2 = x.reshape(tokens, d)
    pos2 = positions.reshape(tokens, 1)
    row = lambda v: v.reshape(1, -1)

    tm = PROJ_TM
    full = lambda shape: pl.BlockSpec(shape, lambda i: (0, 0))
    tok = lambda w: pl.BlockSpec((tm, w), lambda i: (i, 0))
    n_forms = 2 * SWA_KV_HEADS
    q, k, v, qs, ks, vst = pl.pallas_call(
        _proj_kernel,
        grid=(tokens // tm,),
        in_specs=[tok(d), tok(1), full((1, LANES)), full((1, d)), full((IN_PAD, d)),
                  full((1, Q_LORA)), full((Q_LORA, MLA_HEADS * QK_PAD)),
                  full((1, KV_LORA)), full((KV_LORA, MLA_HEADS * (QK_NOPE + V_HEAD)))],
        out_specs=[tok(MLA_HEADS * QK_PAD), tok(MLA_HEADS * QK_PAD), tok(MLA_WIDTH),
                   tok(SWA_WIDTH), tok(n_forms * LANES),
                   pl.BlockSpec((n_forms * LANES, tm), lambda i: (0, i))],
        out_shape=[jax.ShapeDtypeStruct((tokens, MLA_HEADS * QK_PAD), bf16),
                   jax.ShapeDtypeStruct((tokens, MLA_HEADS * QK_PAD), bf16),
                   jax.ShapeDtypeStruct((tokens, MLA_WIDTH), bf16),
                   jax.ShapeDtypeStruct((tokens, SWA_WIDTH), bf16),
                   jax.ShapeDtypeStruct((tokens, n_forms * LANES), bf16),
                   jax.ShapeDtypeStruct((n_forms * LANES, tokens), bf16)],
        compiler_params=_params(("arbitrary",)),
        name="proj",
    )(x2, pos2, inv128, row(attn_norm[0]), win, row(q_a_norm[0]), wq, row(kv_a_norm[0]), wkv)

    hps = MLA_HEADS_PER_STEP
    groups = MLA_HEADS // hps
    mla_steps = batch * groups
    wu_rows = d // mla_steps
    wd_rows = MLP_HIDDEN // mla_steps
    slab = lambda rows, cols: pl.BlockSpec((rows, cols), lambda b, h: (b * groups + h, 0))
    y_mla, wu, wd = pl.pallas_call(
        _mla_kernel,
        grid=(batch, groups),
        in_specs=[pl.BlockSpec((seq, hps * QK_PAD), lambda b, h: (b, h)),
                  pl.BlockSpec((seq, hps * QK_PAD), lambda b, h: (b, h)),
                  pl.BlockSpec((seq, hps * V_HEAD), lambda b, h: (b, h)),
                  slab(wu_rows, MLP_HIDDEN), slab(wd_rows, d)],
        out_specs=[pl.BlockSpec((seq, hps * V_HEAD), lambda b, h: (b, h)),
                   slab(wu_rows, MLP_HIDDEN), slab(wd_rows, d)],
        out_shape=[jax.ShapeDtypeStruct((tokens, MLA_WIDTH), bf16),
                   jax.ShapeDtypeStruct((d, MLP_HIDDEN), bf16),
                   jax.ShapeDtypeStruct((MLP_HIDDEN, d), bf16)],
        scratch_shapes=[pltpu.VMEM((hps, 2, seq, MLA_TQ), f32),
                        pltpu.VMEM((hps, V_HEAD + 16, seq), bf16)],
        compiler_params=_params(("arbitrary", "arbitrary")),
        name="mla",
    )(q, k, v, w_up[0], w_down[0])

    smem = pl.BlockSpec(memory_space=pltpu.SMEM)
    wo_rows = (MLA_WIDTH + SWA_WIDTH) // (batch * nb)
    wo_slab = pl.BlockSpec((wo_rows, d), lambda b, n: (b * nb + n, 0))
    y_swa, wo = pl.pallas_call(
        _swa_kernel,
        grid=(batch, nb),
        in_specs=[smem, smem,
                  pl.BlockSpec((BLOCK, SWA_WIDTH), lambda b, n: (b * nb + n, 0)),
                  pl.BlockSpec((BLOCK, n_forms * LANES),
                               lambda b, n: (b * nb + jnp.maximum(n - 1, 0), 0)),
                  pl.BlockSpec((BLOCK, n_forms * LANES), lambda b, n: (b * nb + n, 0)),
                  pl.BlockSpec((n_forms * LANES, BLOCK),
                               lambda b, n: (0, b * nb + jnp.maximum(n - 1, 0))),
                  pl.BlockSpec((n_forms * LANES, BLOCK), lambda b, n: (0, b * nb + n)),
                  wo_slab],
        out_specs=[pl.BlockSpec((BLOCK, SWA_WIDTH), lambda b, n: (b * nb + n, 0)), wo_slab],
        out_shape=[jax.ShapeDtypeStruct((tokens, SWA_WIDTH), bf16),
                   jax.ShapeDtypeStruct((MLA_WIDTH + SWA_WIDTH, d), bf16)],
        scratch_shapes=[pltpu.VMEM((2, SWA_Q_HEADS, 2 * BLOCK, BLOCK), f32)],
        compiler_params=_params(("arbitrary", "arbitrary")),
        name="swa",
    )(rel_bias, row(sinks[0]), qs, ks, ks, vst, vst, w_out[0])

    tm = OUT_TM
    h1, m = pl.pallas_call(
        _out_kernel,
        grid=(tokens // tm,),
        in_specs=[tok(d), tok(MLA_WIDTH), tok(SWA_WIDTH),
                  pl.BlockSpec((MLA_WIDTH, d), lambda i: (0, 0)),
                  pl.BlockSpec((SWA_WIDTH, d), lambda i: (1, 0)), full((1, d))],
        out_specs=[tok(d), tok(d)],
        out_shape=[jax.ShapeDtypeStruct((tokens, d), f32),
                   jax.ShapeDtypeStruct((tokens, d), bf16)],
        compiler_params=_params(("arbitrary",)),
        name="outproj",
    )(x2, y_mla, y_swa, wo, wo, row(mlp_norm[0]))

    tm, th = MLP_TM, MLP_TH
    assert MLP_HIDDEN // th >= 2
    out = pl.pallas_call(
        _mlp_kernel,
        grid=(tokens // tm, MLP_HIDDEN // th),
        in_specs=[pl.BlockSpec((tm, d), lambda i, j: (i, 0)),
                  pl.BlockSpec(memory_space=pl.ANY),
                  pl.BlockSpec((d, th), lambda i, j: (0, j)),
                  pl.BlockSpec((th, d), lambda i, j: (j, 0)),
                  pl.BlockSpec((1, d), lambda i, j: (0, 0))],
        out_specs=pl.BlockSpec((tm, d), lambda i, j: (i, 0)),
        out_shape=jax.ShapeDtypeStruct((tokens, d), f32),
        scratch_shapes=[pltpu.VMEM((tm, d), f32), pltpu.SemaphoreType.DMA(())],
        compiler_params=pltpu.CompilerParams(
            dimension_semantics=("arbitrary", "arbitrary"), vmem_limit_bytes=MLP_VMEM_LIMIT),
        name="mlp",
    )(m, h1, wu, wd, row(final_norm))

    return out.reshape(batch, seq, d)
```

```python
import jax
import jax.numpy as jnp
import numpy as np
from jax import lax
from jax.experimental import pallas as pl
from jax.experimental.pallas import tpu as pltpu

D_MODEL = 2048
MLA_HEADS = 8
QK_NOPE = 128
QK_ROPE = 64
V_HEAD = 128
Q_LORA = 512
KV_LORA = 512
ROPE_THETA = 10000.0
SWA_Q_HEADS = 16
SWA_KV_HEADS = 2
SWA_GROUP = SWA_Q_HEADS // SWA_KV_HEADS
SWA_HEAD_DIM = 64
WINDOW = 128
BLOCK = 128
REL_BUCKETS = 32
REL_MAX_DIST = 128
MLP_HIDDEN = 4 * D_MODEL
EPS = 1e-6
MLA_WIDTH = MLA_HEADS * V_HEAD
SWA_WIDTH = SWA_Q_HEADS * SWA_HEAD_DIM
SWA_KV_WIDTH = SWA_KV_HEADS * SWA_HEAD_DIM

LANES = 128
QK_PAD = 2 * LANES
LOG2E = float(np.log2(np.e))
MLA_SCALE = (QK_NOPE + QK_ROPE) ** -0.5 * LOG2E
SWA_SCALE = SWA_HEAD_DIM ** -0.5 * LOG2E

_C_Q = 0
_C_KV = _C_Q + Q_LORA
_C_QS = _C_KV + KV_LORA
_C_KS = _C_QS + SWA_WIDTH
_C_VS = _C_KS + SWA_KV_WIDTH
_C_KR = _C_VS + SWA_KV_WIDTH
IN_PAD = _C_KR + LANES

VMEM_LIMIT = 56 * 1024 * 1024

WIN_ROWS = 256
PROJ_TM = 512
MLA_TQ = 256
MLA_HEADS_PER_STEP = 2
SWA_BLOCKS_PER_STEP = 4
OUT_TM = 512
MLP_TM = 1024
MLP_TH = 1024
MLP_LAST_CHUNKS = 2
MLP_VMEM_LIMIT = 60 * 1024 * 1024

bf16 = jnp.bfloat16
f32 = jnp.float32


def _rms(x, g):
    return x * lax.rsqrt(jnp.mean(x * x, axis=-1, keepdims=True) + EPS) * g


def _dot(a, b):
    return jnp.dot(a, b, preferred_element_type=f32)


def _dot_nt(a, b):
    return lax.dot_general(a, b, (((1,), (1,)), ((), ())), preferred_element_type=f32)


def _dot_tn(a, b):
    return lax.dot_general(a, b, (((0,), (0,)), ((), ())), preferred_element_type=f32)


def _rope128(t, cos, sin):
    lane = lax.broadcasted_iota(jnp.int32, t.shape, 1)
    fwd = pltpu.roll(t, QK_ROPE // 2, axis=1)
    bwd = pltpu.roll(t, LANES - QK_ROPE // 2, axis=1)
    rot = jnp.where((lane & (QK_ROPE // 2)) == 0, -bwd, fwd)
    return t * cos + rot * sin


_SRC_KR = Q_LORA + KV_LORA
_SRC_QS = _SRC_KR + QK_ROPE
_SRC_KS = _SRC_QS + SWA_WIDTH
_SRC_END = _SRC_KS + 2 * SWA_KV_WIDTH


def _win_kernel(w_ref, wq_ref, wkv_ref, o_ref, wq_o_ref, wkv_o_ref):
    wkv_o_ref[...] = wkv_ref[...].astype(bf16)
    hd = QK_NOPE + QK_ROPE
    for h in range(MLA_HEADS):
        wq_o_ref[:, h * QK_PAD:h * QK_PAD + hd] = wq_ref[:, h * hd:(h + 1) * hd].astype(bf16)
        wq_o_ref[:, h * QK_PAD + hd:(h + 1) * QK_PAD] = jnp.zeros(
            (wq_o_ref.shape[0], QK_PAD - hd), bf16)
    o_ref[_C_Q:_C_QS, :] = w_ref[0:_SRC_KR, :].astype(bf16)
    o_ref[_C_QS:_C_KS, :] = w_ref[_SRC_QS:_SRC_KS, :].astype(bf16)
    o_ref[_C_KS:_C_KR, :] = w_ref[_SRC_KS:_SRC_END, :].astype(bf16)
    o_ref[_C_KR:_C_KR + QK_ROPE, :] = w_ref[_SRC_KR:_SRC_QS, :].astype(bf16)
    o_ref[_C_KR + QK_ROPE:IN_PAD, :] = jnp.zeros(
        (IN_PAD - _C_KR - QK_ROPE, o_ref.shape[1]), bf16)


def _proj_kernel(x_ref, pos_ref, inv_ref, gin_ref, win_ref, gq_ref, wq_ref, gkv_ref, wkv_ref,
                 q_ref, k_ref, v_ref, qs_ref, ks_ref, vst_ref):
    a = _rms(x_ref[...], gin_ref[...])
    proj = _dot_nt(a.astype(bf16), win_ref[...])

    ang = pos_ref[...].astype(f32) * inv_ref[...]
    cos = jnp.cos(ang)
    sin = jnp.sin(ang)

    tm = proj.shape[0]
    qs_ref[...] = (proj[:, _C_QS:_C_QS + SWA_WIDTH] * SWA_SCALE).astype(bf16)
    ks_ref[...] = proj[:, _C_KS:_C_KS + LANES].astype(bf16)
    vt = proj[:, _C_VS:_C_VS + LANES].T.astype(bf16)
    ones = jnp.ones((SWA_HEAD_DIM, tm), bf16)
    for kvh in range(SWA_KV_HEADS):
        v_t = vt[kvh * SWA_HEAD_DIM:(kvh + 1) * SWA_HEAD_DIM]
        vst_ref[kvh * LANES:(kvh + 1) * LANES, :] = jnp.concatenate([v_t, ones], axis=0)

    cq = _rms(proj[:, _C_Q:_C_Q + Q_LORA], gq_ref[...])
    qf = _dot(cq.astype(bf16), wq_ref[...])
    ckv = _rms(proj[:, _C_KV:_C_KV + KV_LORA], gkv_ref[...])
    kvf = _dot(ckv.astype(bf16), wkv_ref[...])
    kr = _rope128(proj[:, _C_KR:_C_KR + LANES], cos, sin).astype(bf16)
    for h in range(MLA_HEADS):
        c0 = h * QK_PAD
        q_ref[:, c0:c0 + LANES] = (qf[:, c0:c0 + LANES] * MLA_SCALE).astype(bf16)
        q_ref[:, c0 + LANES:c0 + QK_PAD] = (
            _rope128(qf[:, c0 + LANES:c0 + QK_PAD], cos, sin) * MLA_SCALE).astype(bf16)
        k_ref[:, c0:c0 + LANES] = kvf[:, c0:c0 + LANES].astype(bf16)
        k_ref[:, c0 + LANES:c0 + QK_PAD] = kr
        v_ref[:, h * V_HEAD:(h + 1) * V_HEAD] = kvf[:, c0 + LANES:c0 + QK_PAD].astype(bf16)


def _mla_kernel(q_ref, k_ref, v_ref, wu_ref, wd_ref, o_ref, wu_o_ref, wd_o_ref, s_ref, vt_ref):
    wu_o_ref[...] = wu_ref[...].astype(bf16)
    wd_o_ref[...] = wd_ref[...].astype(bf16)

    seq = q_ref.shape[0]
    tq = MLA_TQ
    n_tiles = seq // tq
    heads = range(MLA_HEADS_PER_STEP)
    order = list(reversed(range(n_tiles)))

    def scores(g, i):
        l = (i + 1) * tq
        s_ref[g, i % 2, 0:l, :] = _dot_nt(k_ref[0:l, g * QK_PAD:(g + 1) * QK_PAD],
                                          q_ref[i * tq:(i + 1) * tq, g * QK_PAD:(g + 1) * QK_PAD])

    for g in heads:
        vt_ref[g, 0:V_HEAD, :] = v_ref[:, g * V_HEAD:(g + 1) * V_HEAD].T
        vt_ref[g, V_HEAD:, :] = jnp.ones((vt_ref.shape[1] - V_HEAD, seq), bf16)
        scores(g, order[0])
    for step, i in enumerate(order):
        for g in heads:
            if step + 1 < n_tiles:
                scores(g, order[step + 1])
            l0 = i * tq
            key = lax.broadcasted_iota(jnp.int32, (tq, tq), 0)
            qry = lax.broadcasted_iota(jnp.int32, (tq, tq), 1)
            sd = jnp.where(key <= qry, s_ref[g, i % 2, l0:l0 + tq, :], -jnp.inf)
            m = jnp.max(sd, axis=0, keepdims=True)
            if i > 0:
                sa = s_ref[g, i % 2, 0:l0, :]
                m = jnp.maximum(m, jnp.max(sa, axis=0, keepdims=True))
            acc = _dot(vt_ref[g, :, l0:l0 + tq], jnp.exp2(sd - m).astype(bf16))
            if i > 0:
                acc = acc + _dot(vt_ref[g, :, 0:l0], jnp.exp2(sa - m).astype(bf16))
            o_ref[l0:l0 + tq, g * V_HEAD:(g + 1) * V_HEAD] = (
                acc[0:V_HEAD] / acc[V_HEAD:V_HEAD + 1]).T.astype(o_ref.dtype)


def _swa_bias_table(rel_ref, tab_ref):
    kj = lax.broadcasted_iota(jnp.int32, (2 * BLOCK, BLOCK), 0)
    qi = lax.broadcasted_iota(jnp.int32, (2 * BLOCK, BLOCK), 1) + BLOCK
    dist = qi - kj
    in_window = (dist >= 0) & (dist < WINDOW)
    n = jnp.maximum(dist, 0)
    max_exact = REL_BUCKETS // 2
    n_safe = jnp.maximum(n, 1).astype(f32)
    large = max_exact + (jnp.log(n_safe / max_exact) / np.log(REL_MAX_DIST / max_exact)
                         * (REL_BUCKETS - max_exact)).astype(jnp.int32)
    large = jnp.minimum(large, REL_BUCKETS - 1)
    bucket = jnp.where(n < max_exact, n, large)
    for h in range(SWA_Q_HEADS):
        acc = jnp.zeros((2 * BLOCK, BLOCK), f32)
        for b in range(REL_BUCKETS):
            acc = jnp.where(bucket == b, rel_ref[b, h] * LOG2E, acc)
        tab_ref[0, :, h * BLOCK:(h + 1) * BLOCK] = jnp.where(in_window, acc, -jnp.inf)
        tab_ref[1, :, h * BLOCK:(h + 1) * BLOCK] = jnp.where(
            in_window & (kj >= BLOCK), acc, -jnp.inf)


def _swa_kernel(rel_ref, sink_ref, q_ref, kp_ref, kc_ref, vp_ref, vc_ref, wo_ref,
                o_ref, wo_o_ref, tab_ref):
    wo_o_ref[...] = wo_ref[...].astype(bf16)
    first = (pl.program_id(0) == 0) & (pl.program_id(1) == 0)

    @pl.when(first)
    def _():
        _swa_bias_table(rel_ref, tab_ref)

    hd = SWA_HEAD_DIM
    width = SWA_GROUP * BLOCK
    zeros = jnp.zeros((hd, BLOCK), bf16)
    sink = jnp.concatenate(
        [jnp.full((1, BLOCK), sink_ref[0, h] * LOG2E, f32) for h in range(SWA_Q_HEADS)], axis=1)
    k_all = jnp.concatenate([kp_ref[...], kc_ref[...]], axis=0)
    v_all = jnp.concatenate([vp_ref[...], vc_ref[...]], axis=1)
    for j in range(SWA_BLOCKS_PER_STEP):
        rows = slice(j * BLOCK, (j + 1) * BLOCK)
        kband = k_all[j * BLOCK:(j + 2) * BLOCK]
        vband = v_all[:, j * BLOCK:(j + 2) * BLOCK]
        if j == 0:
            tab = tab_ref[(pl.program_id(1) == 0).astype(jnp.int32)]
        else:
            tab = tab_ref[0]
        q_t = q_ref[rows, :].T
        cols = []
        for h in range(SWA_Q_HEADS):
            qh = q_t[h * hd:(h + 1) * hd]
            cols.append(jnp.concatenate([qh, zeros] if h < SWA_GROUP else [zeros, qh], axis=0))
        s = _dot(kband, jnp.concatenate(cols, axis=1)) + tab
        m = jnp.maximum(jnp.max(s, axis=0, keepdims=True), sink)
        p = jnp.exp2(s - m).astype(bf16)
        sink_term = jnp.exp2(sink - m)
        outs = []
        for c in range(SWA_KV_HEADS):
            r = _dot(vband[c * LANES:(c + 1) * LANES, :], p[:, c * width:(c + 1) * width])
            outs.append(r[0:hd] / (r[hd:hd + 1] + sink_term[:, c * width:(c + 1) * width]))
        for pair in range(SWA_Q_HEADS // 2):
            halves = []
            for h in (2 * pair, 2 * pair + 1):
                g = h % SWA_GROUP
                halves.append(outs[h // SWA_GROUP][:, g * BLOCK:(g + 1) * BLOCK])
            o_ref[rows, pair * LANES:(pair + 1) * LANES] = (
                jnp.concatenate(halves, axis=0).T.astype(o_ref.dtype))


def _out_kernel(x_ref, ya_ref, yb_ref, wa_ref, wb_ref, g_ref, h_ref, m_ref):
    h = x_ref[...] + _dot(ya_ref[...], wa_ref[...]) + _dot(yb_ref[...], wb_ref[...])
    h_ref[...] = h
    m_ref[...] = _rms(h, g_ref[...]).astype(m_ref.dtype)


def _mlp_kernel(m_ref, h_hbm, wu_ref, wd_ref, g_ref, o_ref, hbuf, sem):
    i = pl.program_id(0)
    j = pl.program_id(1)
    last_j = pl.num_programs(1) - 1
    tm = o_ref.shape[0]
    h_copy = pltpu.make_async_copy(h_hbm.at[pl.ds(i * tm, tm), :], hbuf, sem)

    def step(first, last):
        n_chunks = MLP_LAST_CHUNKS if last else 1
        rows = tm // n_chunks
        for c in range(n_chunks):
            r = pl.ds(c * rows, rows)
            u = jnp.maximum(_dot(m_ref[r, :], wu_ref[...]), 0.0)
            acc = _dot((u * u).astype(bf16), wd_ref[...])
            if not first:
                acc = o_ref[r, :] + acc
            if last:
                acc = _rms(acc + hbuf[r, :], g_ref[...])
            o_ref[r, :] = acc

    @pl.when(j == 0)
    def _():
        h_copy.start()
        step(first=True, last=False)

    @pl.when((j > 0) & (j < last_j))
    def _():
        step(first=False, last=False)

    @pl.when(j == last_j)
    def _():
        h_copy.wait()
        step(first=False, last=True)


def _params(sem):
    return pltpu.CompilerParams(dimension_semantics=sem, vmem_limit_bytes=VMEM_LIMIT)


def kernel(x, positions, rel_bias, attn_norm, w_in, q_a_norm, w_q_b, kv_a_norm, w_kv_b, sinks,
           w_out, mlp_norm, w_up, w_down, final_norm):
    batch, seq, d = x.shape
    tokens = batch * seq
    nb = seq // BLOCK
    depth = w_in.shape[0]
    assert depth == 1 and d == D_MODEL and seq % MLA_TQ == 0

    assert w_in.shape[2] == _SRC_END
    n_prep = d // WIN_ROWS
    lat_rows = Q_LORA // n_prep
    assert KV_LORA == Q_LORA
    slab3 = lambda rows, cols: pl.BlockSpec((None, rows, cols), lambda i: (0, i, 0))
    slab2 = lambda rows, cols: pl.BlockSpec((rows, cols), lambda i: (i, 0))
    kv_cols = MLA_HEADS * (QK_NOPE + V_HEAD)
    win, wq, wkv = pl.pallas_call(
        _win_kernel,
        grid=(n_prep,),
        in_specs=[pl.BlockSpec((_SRC_END, WIN_ROWS), lambda i: (0, i)),
                  slab3(lat_rows, MLA_HEADS * (QK_NOPE + QK_ROPE)), slab3(lat_rows, kv_cols)],
        out_specs=[pl.BlockSpec((IN_PAD, WIN_ROWS), lambda i: (0, i)),
                   slab2(lat_rows, MLA_HEADS * QK_PAD), slab2(lat_rows, kv_cols)],
        out_shape=[jax.ShapeDtypeStruct((IN_PAD, d), bf16),
                   jax.ShapeDtypeStruct((Q_LORA, MLA_HEADS * QK_PAD), bf16),
                   jax.ShapeDtypeStruct((KV_LORA, kv_cols), bf16)],
        compiler_params=_params(("arbitrary",)),
        name="winprep",
    )(jnp.transpose(w_in[0]), w_q_b, w_kv_b)

    inv = 1.0 / (ROPE_THETA ** (jnp.arange(0, QK_ROPE, 2, dtype=f32) / QK_ROPE))
    inv128 = jnp.concatenate([inv, inv, jnp.zeros((LANES - QK_ROPE,), f32)])[None, :]

    x2 = x.reshape(tokens, d)
    pos2 = positions.reshape(tokens, 1)
    row = lambda v: v.reshape(1, -1)

    tm = PROJ_TM
    full = lambda shape: pl.BlockSpec(shape, lambda i: (0, 0))
    tok = lambda w: pl.BlockSpec((tm, w), lambda i: (i, 0))
    vt_rows = SWA_KV_HEADS * LANES
    q, k, v, qs, ks, vst = pl.pallas_call(
        _proj_kernel,
        grid=(tokens // tm,),
        in_specs=[tok(d), tok(1), full((1, LANES)), full((1, d)), full((IN_PAD, d)),
                  full((1, Q_LORA)), full((Q_LORA, MLA_HEADS * QK_PAD)),
                  full((1, KV_LORA)), full((KV_LORA, MLA_HEADS * (QK_NOPE + V_HEAD)))],
        out_specs=[tok(MLA_HEADS * QK_PAD), tok(MLA_HEADS * QK_PAD), tok(MLA_WIDTH),
                   tok(SWA_WIDTH), tok(SWA_KV_WIDTH),
                   pl.BlockSpec((vt_rows, tm), lambda i: (0, i))],
        out_shape=[jax.ShapeDtypeStruct((tokens, MLA_HEADS * QK_PAD), bf16),
                   jax.ShapeDtypeStruct((tokens, MLA_HEADS * QK_PAD), bf16),
                   jax.ShapeDtypeStruct((tokens, MLA_WIDTH), bf16),
                   jax.ShapeDtypeStruct((tokens, SWA_WIDTH), bf16),
                   jax.ShapeDtypeStruct((tokens, SWA_KV_WIDTH), bf16),
                   jax.ShapeDtypeStruct((vt_rows, tokens), bf16)],
        compiler_params=_params(("arbitrary",)),
        name="proj",
    )(x2, pos2, inv128, row(attn_norm[0]), win, row(q_a_norm[0]), wq, row(kv_a_norm[0]), wkv)

    hps = MLA_HEADS_PER_STEP
    groups = MLA_HEADS // hps
    mla_steps = batch * groups
    wu_rows = d // mla_steps
    wd_rows = MLP_HIDDEN // mla_steps
    slab = lambda rows, cols: pl.BlockSpec((rows, cols), lambda b, h: (b * groups + h, 0))
    y_mla, wu, wd = pl.pallas_call(
        _mla_kernel,
        grid=(batch, groups),
        in_specs=[pl.BlockSpec((seq, hps * QK_PAD), lambda b, h: (b, h)),
                  pl.BlockSpec((seq, hps * QK_PAD), lambda b, h: (b, h)),
                  pl.BlockSpec((seq, hps * V_HEAD), lambda b, h: (b, h)),
                  slab(wu_rows, MLP_HIDDEN), slab(wd_rows, d)],
        out_specs=[pl.BlockSpec((seq, hps * V_HEAD), lambda b, h: (b, h)),
                   slab(wu_rows, MLP_HIDDEN), slab(wd_rows, d)],
        out_shape=[jax.ShapeDtypeStruct((tokens, MLA_WIDTH), bf16),
                   jax.ShapeDtypeStruct((d, MLP_HIDDEN), bf16),
                   jax.ShapeDtypeStruct((MLP_HIDDEN, d), bf16)],
        scratch_shapes=[pltpu.VMEM((hps, 2, seq, MLA_TQ), f32),
                        pltpu.VMEM((hps, V_HEAD + 16, seq), bf16)],
        compiler_params=_params(("arbitrary", "arbitrary")),
        name="mla",
    )(q, k, v, w_up[0], w_down[0])

    smem = pl.BlockSpec(memory_space=pltpu.SMEM)
    bps = SWA_BLOCKS_PER_STEP
    ns = nb // bps
    rows = bps * BLOCK
    wo_rows = (MLA_WIDTH + SWA_WIDTH) // (batch * ns)
    wo_slab = pl.BlockSpec((wo_rows, d), lambda b, n: (b * ns + n, 0))
    prev = lambda b, n: b * nb + jnp.maximum(n * bps - 1, 0)
    y_swa, wo = pl.pallas_call(
        _swa_kernel,
        grid=(batch, ns),
        in_specs=[smem, smem,
                  pl.BlockSpec((rows, SWA_WIDTH), lambda b, n: (b * ns + n, 0)),
                  pl.BlockSpec((BLOCK, SWA_KV_WIDTH), lambda b, n: (prev(b, n), 0)),
                  pl.BlockSpec((rows, SWA_KV_WIDTH), lambda b, n: (b * ns + n, 0)),
                  pl.BlockSpec((vt_rows, BLOCK), lambda b, n: (0, prev(b, n))),
                  pl.BlockSpec((vt_rows, rows), lambda b, n: (0, b * ns + n)),
                  wo_slab],
        out_specs=[pl.BlockSpec((rows, SWA_WIDTH), lambda b, n: (b * ns + n, 0)), wo_slab],
        out_shape=[jax.ShapeDtypeStruct((tokens, SWA_WIDTH), bf16),
                   jax.ShapeDtypeStruct((MLA_WIDTH + SWA_WIDTH, d), bf16)],
        scratch_shapes=[pltpu.VMEM((2, 2 * BLOCK, SWA_Q_HEADS * BLOCK), f32)],
        compiler_params=_params(("arbitrary", "arbitrary")),
        name="swa",
    )(rel_bias, row(sinks[0]), qs, ks, ks, vst, vst, w_out[0])

    tm = OUT_TM
    h1, m = pl.pallas_call(
        _out_kernel,
        grid=(tokens // tm,),
        in_specs=[tok(d), tok(MLA_WIDTH), tok(SWA_WIDTH),
                  pl.BlockSpec((MLA_WIDTH, d), lambda i: (0, 0)),
                  pl.BlockSpec((SWA_WIDTH, d), lambda i: (1, 0)), full((1, d))],
        out_specs=[tok(d), tok(d)],
        out_shape=[jax.ShapeDtypeStruct((tokens, d), f32),
                   jax.ShapeDtypeStruct((tokens, d), bf16)],
        compiler_params=_params(("arbitrary",)),
        name="outproj",
    )(x2, y_mla, y_swa, wo, wo, row(mlp_norm[0]))

    tm, th = MLP_TM, MLP_TH
    assert MLP_HIDDEN // th >= 2
    out = pl.pallas_call(
        _mlp_kernel,
        grid=(tokens // tm, MLP_HIDDEN // th),
        in_specs=[pl.BlockSpec((tm, d), lambda i, j: (i, 0)),
                  pl.BlockSpec(memory_space=pl.ANY),
                  pl.BlockSpec((d, th), lambda i, j: (0, j)),
                  pl.BlockSpec((th, d), lambda i, j: (j, 0)),
                  pl.BlockSpec((1, d), lambda i, j: (0, 0))],
        out_specs=pl.BlockSpec((tm, d), lambda i, j: (i, 0)),
        out_shape=jax.ShapeDtypeStruct((tokens, d), f32),
        scratch_shapes=[pltpu.VMEM((tm, d), f32), pltpu.SemaphoreType.DMA(())],
        compiler_params=pltpu.CompilerParams(
            dimension_semantics=("arbitrary", "arbitrary"), vmem_limit_bytes=MLP_VMEM_LIMIT),
        name="mlp",
    )(m, h1, wu, wd, row(final_norm))

    return out.reshape(batch, seq, d)
```

```python
import jax
import jax.numpy as jnp
import numpy as np
from jax import lax
from jax.experimental import pallas as pl
from jax.experimental.pallas import tpu as pltpu

D_MODEL = 2048
MLA_HEADS = 8
QK_NOPE = 128
QK_ROPE = 64
V_HEAD = 128
Q_LORA = 512
KV_LORA = 512
ROPE_THETA = 10000.0
SWA_Q_HEADS = 16
SWA_KV_HEADS = 2
SWA_GROUP = SWA_Q_HEADS // SWA_KV_HEADS
SWA_HEAD_DIM = 64
WINDOW = 128
BLOCK = 128
REL_BUCKETS = 32
REL_MAX_DIST = 128
MLP_HIDDEN = 4 * D_MODEL
EPS = 1e-6
MLA_WIDTH = MLA_HEADS * V_HEAD
SWA_WIDTH = SWA_Q_HEADS * SWA_HEAD_DIM
SWA_KV_WIDTH = SWA_KV_HEADS * SWA_HEAD_DIM

LANES = 128
QK_PAD = 2 * LANES
VT_ROWS = V_HEAD + 16
LOG2E = float(np.log2(np.e))
MLA_SCALE = (QK_NOPE + QK_ROPE) ** -0.5 * LOG2E
SWA_SCALE = SWA_HEAD_DIM ** -0.5 * LOG2E

_C_Q = 0
_C_KV = _C_Q + Q_LORA
_C_QS = _C_KV + KV_LORA
_C_KS = _C_QS + SWA_WIDTH
_C_VS = _C_KS + SWA_KV_WIDTH
_C_KR = _C_VS + SWA_KV_WIDTH
IN_PAD = _C_KR + LANES

def _t5_bucket_starts():
    max_exact = REL_BUCKETS // 2
    dist = np.arange(WINDOW)
    val = (np.log(np.maximum(dist, 1).astype(np.float32) / np.float32(max_exact))
           / np.float32(np.log(REL_MAX_DIST / max_exact)) * np.float32(REL_BUCKETS - max_exact))
    margin = np.abs(val - np.round(val))[max_exact + 1:]
    assert margin.min() > 1e-3, margin.min()
    large = np.minimum(max_exact + np.floor(val).astype(np.int64), REL_BUCKETS - 1)
    bucket = np.where(dist < max_exact, dist, large)
    assert np.all(np.diff(bucket) >= 0)
    return tuple(int(np.argmax(bucket >= b)) if np.any(bucket >= b) else WINDOW
                 for b in range(REL_BUCKETS))


T5_BUCKET_START = _t5_bucket_starts()

VMEM_LIMIT = 56 * 1024 * 1024

WIN_ROWS = 256
PROJ_TM = 512
MLA_TQ = 256
MLA_HEADS_PER_STEP = 2
SWA_BLOCKS_PER_STEP = 4
OUT_TM = 512
MLP_TM = 1024
MLP_TH = 1024
MLP_LAST_CHUNKS = 2
MLP_VMEM_LIMIT = 60 * 1024 * 1024

bf16 = jnp.bfloat16
f32 = jnp.float32


def _rms(x, g):
    return x * lax.rsqrt(jnp.mean(x * x, axis=-1, keepdims=True) + EPS) * g


def _dot(a, b):
    return jnp.dot(a, b, preferred_element_type=f32)


def _dot_nt(a, b):
    return lax.dot_general(a, b, (((1,), (1,)), ((), ())), preferred_element_type=f32)


def _dot_tn(a, b):
    return lax.dot_general(a, b, (((0,), (0,)), ((), ())), preferred_element_type=f32)


def _rope128(t, cos, sin):
    lane = lax.broadcasted_iota(jnp.int32, t.shape, 1)
    fwd = pltpu.roll(t, QK_ROPE // 2, axis=1)
    bwd = pltpu.roll(t, LANES - QK_ROPE // 2, axis=1)
    rot = jnp.where((lane & (QK_ROPE // 2)) == 0, -bwd, fwd)
    return t * cos + rot * sin


_SRC_KR = Q_LORA + KV_LORA
_SRC_QS = _SRC_KR + QK_ROPE
_SRC_KS = _SRC_QS + SWA_WIDTH
_SRC_END = _SRC_KS + 2 * SWA_KV_WIDTH


def _win_kernel(w_ref, wq_ref, wkv_ref, o_ref, wq_o_ref, wkv_o_ref):
    wkv_o_ref[...] = wkv_ref[...].astype(bf16)
    hd = QK_NOPE + QK_ROPE
    for h in range(MLA_HEADS):
        wq_o_ref[:, h * QK_PAD:h * QK_PAD + hd] = wq_ref[:, h * hd:(h + 1) * hd].astype(bf16)
        wq_o_ref[:, h * QK_PAD + hd:(h + 1) * QK_PAD] = jnp.zeros(
            (wq_o_ref.shape[0], QK_PAD - hd), bf16)
    o_ref[_C_Q:_C_QS, :] = w_ref[0:_SRC_KR, :].astype(bf16)
    o_ref[_C_QS:_C_KS, :] = w_ref[_SRC_QS:_SRC_KS, :].astype(bf16)
    o_ref[_C_KS:_C_KR, :] = w_ref[_SRC_KS:_SRC_END, :].astype(bf16)
    o_ref[_C_KR:_C_KR + QK_ROPE, :] = w_ref[_SRC_KR:_SRC_QS, :].astype(bf16)
    o_ref[_C_KR + QK_ROPE:IN_PAD, :] = jnp.zeros(
        (IN_PAD - _C_KR - QK_ROPE, o_ref.shape[1]), bf16)


def _proj_kernel(x_ref, pos_ref, inv_ref, gin_ref, win_ref, gq_ref, wq_ref, gkv_ref, wkv_ref,
                 wd_ref, q_ref, k_ref, v_ref, qs_ref, ks_ref, vst_ref, wd_o_ref):
    wd_o_ref[...] = wd_ref[...].astype(bf16)
    a = _rms(x_ref[...], gin_ref[...])
    proj = _dot_nt(a.astype(bf16), win_ref[...])

    ang = pos_ref[...].astype(f32) * inv_ref[...]
    cos = jnp.cos(ang)
    sin = jnp.sin(ang)

    tm = proj.shape[0]
    qs_ref[...] = (proj[:, _C_QS:_C_QS + SWA_WIDTH] * SWA_SCALE).astype(bf16)
    ks_ref[...] = proj[:, _C_KS:_C_KS + LANES].astype(bf16)
    vt = proj[:, _C_VS:_C_VS + LANES].T.astype(bf16)
    ones = jnp.ones((SWA_HEAD_DIM, tm), bf16)
    for kvh in range(SWA_KV_HEADS):
        v_t = vt[kvh * SWA_HEAD_DIM:(kvh + 1) * SWA_HEAD_DIM]
        vst_ref[kvh * LANES:(kvh + 1) * LANES, :] = jnp.concatenate([v_t, ones], axis=0)

    cq = _rms(proj[:, _C_Q:_C_Q + Q_LORA], gq_ref[...])
    qf = _dot(cq.astype(bf16), wq_ref[...])
    ckv = _rms(proj[:, _C_KV:_C_KV + KV_LORA], gkv_ref[...])
    kvf = _dot(ckv.astype(bf16), wkv_ref[...])
    kr = _rope128(proj[:, _C_KR:_C_KR + LANES], cos, sin).astype(bf16)
    for h in range(MLA_HEADS):
        c0 = h * QK_PAD
        q_ref[:, c0:c0 + LANES] = (qf[:, c0:c0 + LANES] * MLA_SCALE).astype(bf16)
        q_ref[:, c0 + LANES:c0 + QK_PAD] = (
            _rope128(qf[:, c0 + LANES:c0 + QK_PAD], cos, sin) * MLA_SCALE).astype(bf16)
        k_ref[:, c0:c0 + LANES] = kvf[:, c0:c0 + LANES].astype(bf16)
        k_ref[:, c0 + LANES:c0 + QK_PAD] = kr
        v_ref[:, h * V_HEAD:(h + 1) * V_HEAD] = kvf[:, c0 + LANES:c0 + QK_PAD].astype(bf16)


def _mla_kernel(q_ref, k_ref, v_ref, wu_ref, o_ref, wu_o_ref, s_ref, vt_ref):
    wu_o_ref[...] = wu_ref[...].astype(bf16)

    seq = q_ref.shape[0]
    tq = MLA_TQ
    n_tiles = seq // tq
    heads = range(MLA_HEADS_PER_STEP)
    order = list(reversed(range(n_tiles)))

    def scores(g, i):
        l = (i + 1) * tq
        s_ref[g, i % 2, 0:l, :] = _dot_nt(k_ref[0:l, g * QK_PAD:(g + 1) * QK_PAD],
                                          q_ref[i * tq:(i + 1) * tq, g * QK_PAD:(g + 1) * QK_PAD])

    for g in heads:
        vt_ref[g, 0:V_HEAD, :] = v_ref[:, g * V_HEAD:(g + 1) * V_HEAD].T
        vt_ref[g, V_HEAD:, :] = jnp.ones((VT_ROWS - V_HEAD, seq), bf16)
        scores(g, order[0])
    for step, i in enumerate(order):
        for g in heads:
            if step + 1 < n_tiles:
                scores(g, order[step + 1])
            l0 = i * tq
            key = lax.broadcasted_iota(jnp.int32, (tq, tq), 0)
            qry = lax.broadcasted_iota(jnp.int32, (tq, tq), 1)
            sd = jnp.where(key <= qry, s_ref[g, i % 2, l0:l0 + tq, :], -jnp.inf)
            m = jnp.max(sd, axis=0, keepdims=True)
            if i > 0:
                sa = s_ref[g, i % 2, 0:l0, :]
                m = jnp.maximum(m, jnp.max(sa, axis=0, keepdims=True))
            acc = _dot(vt_ref[g, :, l0:l0 + tq], jnp.exp2(sd - m).astype(bf16))
            if i > 0:
                acc = acc + _dot(vt_ref[g, :, 0:l0], jnp.exp2(sa - m).astype(bf16))
            o_ref[l0:l0 + tq, g * V_HEAD:(g + 1) * V_HEAD] = (
                acc[0:V_HEAD] / acc[V_HEAD:V_HEAD + 1]).T.astype(o_ref.dtype)


def _swa_bias_table(rel_ref, tab_ref):
    kj = lax.broadcasted_iota(jnp.int32, (2 * BLOCK, BLOCK), 0)
    qi = lax.broadcasted_iota(jnp.int32, (2 * BLOCK, BLOCK), 1) + BLOCK
    dist = qi - kj
    in_window = (dist >= 0) & (dist < WINDOW)
    for h in range(SWA_Q_HEADS):
        acc = jnp.full((2 * BLOCK, BLOCK), rel_ref[0, h] * LOG2E, f32)
        for b in range(1, REL_BUCKETS):
            if T5_BUCKET_START[b] < WINDOW:
                acc = jnp.where(dist >= T5_BUCKET_START[b], rel_ref[b, h] * LOG2E, acc)
        tab_ref[0, :, h * BLOCK:(h + 1) * BLOCK] = jnp.where(in_window, acc, -jnp.inf)
        tab_ref[1, :, h * BLOCK:(h + 1) * BLOCK] = jnp.where(
            in_window & (kj >= BLOCK), acc, -jnp.inf)


def _swa_kernel(rel_ref, sink_ref, q_ref, kp_ref, kc_ref, vp_ref, vc_ref, wo_ref,
                o_ref, wo_o_ref, tab_ref):
    wo_o_ref[...] = wo_ref[...].astype(bf16)
    first = (pl.program_id(0) == 0) & (pl.program_id(1) == 0)

    @pl.when(first)
    def _():
        _swa_bias_table(rel_ref, tab_ref)

    hd = SWA_HEAD_DIM
    width = SWA_GROUP * BLOCK
    zeros = jnp.zeros((hd, BLOCK), bf16)
    sink = jnp.concatenate(
        [jnp.full((1, BLOCK), sink_ref[0, h] * LOG2E, f32) for h in range(SWA_Q_HEADS)], axis=1)
    k_all = jnp.concatenate([kp_ref[...], kc_ref[...]], axis=0)
    v_all = jnp.concatenate([vp_ref[...], vc_ref[...]], axis=1)
    for j in range(SWA_BLOCKS_PER_STEP):
        rows = slice(j * BLOCK, (j + 1) * BLOCK)
        kband = k_all[j * BLOCK:(j + 2) * BLOCK]
        vband = v_all[:, j * BLOCK:(j + 2) * BLOCK]
        if j == 0:
            tab = tab_ref[(pl.program_id(1) == 0).astype(jnp.int32)]
        else:
            tab = tab_ref[0]
        q_t = q_ref[rows, :].T
        cols = []
        for h in range(SWA_Q_HEADS):
            qh = q_t[h * hd:(h + 1) * hd]
            cols.append(jnp.concatenate([qh, zeros] if h < SWA_GROUP else [zeros, qh], axis=0))
        s = _dot(kband, jnp.concatenate(cols, axis=1)) + tab
        m = jnp.maximum(jnp.max(s, axis=0, keepdims=True), sink)
        p = jnp.exp2(s - m).astype(bf16)
        sink_term = jnp.exp2(sink - m)
        outs = []
        for c in range(SWA_KV_HEADS):
            r = _dot(vband[c * LANES:(c + 1) * LANES, :], p[:, c * width:(c + 1) * width])
            outs.append(r[0:hd] / (r[hd:hd + 1] + sink_term[:, c * width:(c + 1) * width]))
        for pair in range(SWA_Q_HEADS // 2):
            halves = []
            for h in (2 * pair, 2 * pair + 1):
                g = h % SWA_GROUP
                halves.append(outs[h // SWA_GROUP][:, g * BLOCK:(g + 1) * BLOCK])
            o_ref[rows, pair * LANES:(pair + 1) * LANES] = (
                jnp.concatenate(halves, axis=0).T.astype(o_ref.dtype))


def _out_kernel(x_ref, ya_ref, yb_ref, wa_ref, wb_ref, g_ref, h_ref, m_ref):
    h = x_ref[...] + _dot(ya_ref[...], wa_ref[...]) + _dot(yb_ref[...], wb_ref[...])
    h_ref[...] = h
    m_ref[...] = _rms(h, g_ref[...]).astype(m_ref.dtype)


def _mlp_kernel(m_ref, h_hbm, wu_ref, wd_ref, g_ref, o_ref, hbuf, sem):
    i = pl.program_id(0)
    j = pl.program_id(1)
    last_j = pl.num_programs(1) - 1
    tm = o_ref.shape[0]
    h_copy = pltpu.make_async_copy(h_hbm.at[pl.ds(i * tm, tm), :], hbuf, sem)

    def step(first, last):
        n_chunks = MLP_LAST_CHUNKS if last else 1
        rows = tm // n_chunks
        for c in range(n_chunks):
            r = pl.ds(c * rows, rows)
            u = jnp.maximum(_dot(m_ref[r, :], wu_ref[...]), 0.0)
            acc = _dot((u * u).astype(bf16), wd_ref[...])
            if not first:
                acc = o_ref[r, :] + acc
            if last:
                acc = _rms(acc + hbuf[r, :], g_ref[...])
            o_ref[r, :] = acc

    @pl.when(j == 0)
    def _():
        h_copy.start()
        step(first=True, last=False)

    @pl.when((j > 0) & (j < last_j))
    def _():
        step(first=False, last=False)

    @pl.when(j == last_j)
    def _():
        h_copy.wait()
        step(first=False, last=True)


def _params(sem):
    return pltpu.CompilerParams(dimension_semantics=sem, vmem_limit_bytes=VMEM_LIMIT)


def kernel(x, positions, rel_bias, attn_norm, w_in, q_a_norm, w_q_b, kv_a_norm, w_kv_b, sinks,
           w_out, mlp_norm, w_up, w_down, final_norm):
    batch, seq, d = x.shape
    tokens = batch * seq
    nb = seq // BLOCK
    depth = w_in.shape[0]
    assert depth == 1 and d == D_MODEL and seq % MLA_TQ == 0

    assert w_in.shape[2] == _SRC_END
    n_prep = d // WIN_ROWS
    lat_rows = Q_LORA // n_prep
    assert KV_LORA == Q_LORA
    slab3 = lambda rows, cols: pl.BlockSpec((None, rows, cols), lambda i: (0, i, 0))
    slab2 = lambda rows, cols: pl.BlockSpec((rows, cols), lambda i: (i, 0))
    kv_cols = MLA_HEADS * (QK_NOPE + V_HEAD)
    win, wq, wkv = pl.pallas_call(
        _win_kernel,
        grid=(n_prep,),
        in_specs=[pl.BlockSpec((_SRC_END, WIN_ROWS), lambda i: (0, i)),
                  slab3(lat_rows, MLA_HEADS * (QK_NOPE + QK_ROPE)), slab3(lat_rows, kv_cols)],
        out_specs=[pl.BlockSpec((IN_PAD, WIN_ROWS), lambda i: (0, i)),
                   slab2(lat_rows, MLA_HEADS * QK_PAD), slab2(lat_rows, kv_cols)],
        out_shape=[jax.ShapeDtypeStruct((IN_PAD, d), bf16),
                   jax.ShapeDtypeStruct((Q_LORA, MLA_HEADS * QK_PAD), bf16),
                   jax.ShapeDtypeStruct((KV_LORA, kv_cols), bf16)],
        compiler_params=_params(("arbitrary",)),
        name="winprep",
    )(jnp.transpose(w_in[0]), w_q_b, w_kv_b)

    inv = 1.0 / (ROPE_THETA ** (jnp.arange(0, QK_ROPE, 2, dtype=f32) / QK_ROPE))
    inv128 = jnp.concatenate([inv, inv, jnp.zeros((LANES - QK_ROPE,), f32)])[None, :]

    x2 = x.reshape(tokens, d)
    pos2 = positions.reshape(tokens, 1)
    row = lambda v: v.reshape(1, -1)

    tm = PROJ_TM
    full = lambda shape: pl.BlockSpec(shape, lambda i: (0, 0))
    tok = lambda w: pl.BlockSpec((tm, w), lambda i: (i, 0))
    vt_rows = SWA_KV_HEADS * LANES
    wd_slab = pl.BlockSpec((MLP_HIDDEN // (tokens // tm), d), lambda i: (i, 0))
    q, k, v, qs, ks, vst, wd = pl.pallas_call(
        _proj_kernel,
        grid=(tokens // tm,),
        in_specs=[tok(d), tok(1), full((1, LANES)), full((1, d)), full((IN_PAD, d)),
                  full((1, Q_LORA)), full((Q_LORA, MLA_HEADS * QK_PAD)),
                  full((1, KV_LORA)), full((KV_LORA, MLA_HEADS * (QK_NOPE + V_HEAD))),
                  wd_slab],
        out_specs=[tok(MLA_HEADS * QK_PAD), tok(MLA_HEADS * QK_PAD), tok(MLA_WIDTH),
                   tok(SWA_WIDTH), tok(SWA_KV_WIDTH),
                   pl.BlockSpec((vt_rows, tm), lambda i: (0, i)), wd_slab],
        out_shape=[jax.ShapeDtypeStruct((tokens, MLA_HEADS * QK_PAD), bf16),
                   jax.ShapeDtypeStruct((tokens, MLA_HEADS * QK_PAD), bf16),
                   jax.ShapeDtypeStruct((tokens, MLA_WIDTH), bf16),
                   jax.ShapeDtypeStruct((tokens, SWA_WIDTH), bf16),
                   jax.ShapeDtypeStruct((tokens, SWA_KV_WIDTH), bf16),
                   jax.ShapeDtypeStruct((vt_rows, tokens), bf16),
                   jax.ShapeDtypeStruct((MLP_HIDDEN, d), bf16)],
        compiler_params=_params(("arbitrary",)),
        name="proj",
    )(x2, pos2, inv128, row(attn_norm[0]), win, row(q_a_norm[0]), wq, row(kv_a_norm[0]), wkv,
      w_down[0])

    hps = MLA_HEADS_PER_STEP
    groups = MLA_HEADS // hps
    mla_steps = batch * groups
    wu_slab = pl.BlockSpec((d // mla_steps, MLP_HIDDEN), lambda b, h: (b * groups + h, 0))
    y_mla, wu = pl.pallas_call(
        _mla_kernel,
        grid=(batch, groups),
        in_specs=[pl.BlockSpec((seq, hps * QK_PAD), lambda b, h: (b, h)),
                  pl.BlockSpec((seq, hps * QK_PAD), lambda b, h: (b, h)),
                  pl.BlockSpec((seq, hps * V_HEAD), lambda b, h: (b, h)),
                  wu_slab],
        out_specs=[pl.BlockSpec((seq, hps * V_HEAD), lambda b, h: (b, h)), wu_slab],
        out_shape=[jax.ShapeDtypeStruct((tokens, MLA_WIDTH), bf16),
                   jax.ShapeDtypeStruct((d, MLP_HIDDEN), bf16)],
        scratch_shapes=[pltpu.VMEM((hps, 2, seq, MLA_TQ), f32),
                        pltpu.VMEM((hps, VT_ROWS, seq), bf16)],
        compiler_params=_params(("arbitrary", "arbitrary")),
        name="mla",
    )(q, k, v, w_up[0])

    smem = pl.BlockSpec(memory_space=pltpu.SMEM)
    bps = SWA_BLOCKS_PER_STEP
    ns = nb // bps
    rows = bps * BLOCK
    wo_rows = (MLA_WIDTH + SWA_WIDTH) // (batch * ns)
    wo_slab = pl.BlockSpec((wo_rows, d), lambda b, n: (b * ns + n, 0))
    prev = lambda b, n: b * nb + jnp.maximum(n * bps - 1, 0)
    y_swa, wo = pl.pallas_call(
        _swa_kernel,
        grid=(batch, ns),
        in_specs=[smem, smem,
                  pl.BlockSpec((rows, SWA_WIDTH), lambda b, n: (b * ns + n, 0)),
                  pl.BlockSpec((BLOCK, SWA_KV_WIDTH), lambda b, n: (prev(b, n), 0)),
                  pl.BlockSpec((rows, SWA_KV_WIDTH), lambda b, n: (b * ns + n, 0)),
                  pl.BlockSpec((vt_rows, BLOCK), lambda b, n: (0, prev(b, n))),
                  pl.BlockSpec((vt_rows, rows), lambda b, n: (0, b * ns + n)),
                  wo_slab],
        out_specs=[pl.BlockSpec((rows, SWA_WIDTH), lambda b, n: (b * ns + n, 0)), wo_slab],
        out_shape=[jax.ShapeDtypeStruct((tokens, SWA_WIDTH), bf16),
                   jax.ShapeDtypeStruct((MLA_WIDTH + SWA_WIDTH, d), bf16)],
        scratch_shapes=[pltpu.VMEM((2, 2 * BLOCK, SWA_Q_HEADS * BLOCK), f32)],
        compiler_params=_params(("arbitrary", "arbitrary")),
        name="swa",
    )(rel_bias, row(sinks[0]), qs, ks, ks, vst, vst, w_out[0])

    tm = OUT_TM
    h1, m = pl.pallas_call(
        _out_kernel,
        grid=(tokens // tm,),
        in_specs=[tok(d), tok(MLA_WIDTH), tok(SWA_WIDTH),
                  pl.BlockSpec((MLA_WIDTH, d), lambda i: (0, 0)),
                  pl.BlockSpec((SWA_WIDTH, d), lambda i: (1, 0)), full((1, d))],
        out_specs=[tok(d), tok(d)],
        out_shape=[jax.ShapeDtypeStruct((tokens, d), f32),
                   jax.ShapeDtypeStruct((tokens, d), bf16)],
        compiler_params=_params(("arbitrary",)),
        name="outproj",
    )(x2, y_mla, y_swa, wo, wo, row(mlp_norm[0]))

    tm, th = MLP_TM, MLP_TH
    assert MLP_HIDDEN // th >= 2
    out = pl.pallas_call(
        _mlp_kernel,
        grid=(tokens // tm, MLP_HIDDEN // th),
        in_specs=[pl.BlockSpec((tm, d), lambda i, j: (i, 0)),
                  pl.BlockSpec(memory_space=pl.ANY),
                  pl.BlockSpec((d, th), lambda i, j: (0, j)),
                  pl.BlockSpec((th, d), lambda i, j: (j, 0)),
                  pl.BlockSpec((1, d), lambda i, j: (0, 0))],
        out_specs=pl.BlockSpec((tm, d), lambda i, j: (i, 0)),
        out_shape=jax.ShapeDtypeStruct((tokens, d), f32),
        scratch_shapes=[pltpu.VMEM((tm, d), f32), pltpu.SemaphoreType.DMA(())],
        compiler_params=pltpu.CompilerParams(
            dimension_semantics=("arbitrary", "arbitrary"), vmem_limit_bytes=MLP_VMEM_LIMIT),
        name="mlp",
    )(m, h1, wu, wd, row(final_norm))

    return out.reshape(batch, seq, d)
```

```python
import jax
import jax.numpy as jnp
import numpy as np
from jax import lax
from jax.experimental import pallas as pl
from jax.experimental.pallas import tpu as pltpu

D_MODEL = 2048
MLA_HEADS = 8
QK_NOPE = 128
QK_ROPE = 64
V_HEAD = 128
Q_LORA = 512
KV_LORA = 512
ROPE_THETA = 10000.0
SWA_Q_HEADS = 16
SWA_KV_HEADS = 2
SWA_GROUP = SWA_Q_HEADS // SWA_KV_HEADS
SWA_HEAD_DIM = 64
WINDOW = 128
BLOCK = 128
REL_BUCKETS = 32
REL_MAX_DIST = 128
MLP_HIDDEN = 4 * D_MODEL
EPS = 1e-6
MLA_WIDTH = MLA_HEADS * V_HEAD
SWA_WIDTH = SWA_Q_HEADS * SWA_HEAD_DIM
SWA_KV_WIDTH = SWA_KV_HEADS * SWA_HEAD_DIM

LANES = 128
QK_PAD = 2 * LANES
VT_ROWS = V_HEAD + 16
LOG2E = float(np.log2(np.e))
MLA_SCALE = (QK_NOPE + QK_ROPE) ** -0.5 * LOG2E
SWA_SCALE = SWA_HEAD_DIM ** -0.5 * LOG2E

_C_Q = 0
_C_KV = _C_Q + Q_LORA
_C_QS = _C_KV + KV_LORA
_C_KS = _C_QS + SWA_WIDTH
_C_VS = _C_KS + SWA_KV_WIDTH
_C_KR = _C_VS + SWA_KV_WIDTH
IN_PAD = _C_KR + LANES

def _t5_bucket_starts():
    max_exact = REL_BUCKETS // 2
    dist = np.arange(WINDOW)
    val = (np.log(np.maximum(dist, 1).astype(np.float32) / np.float32(max_exact))
           / np.float32(np.log(REL_MAX_DIST / max_exact)) * np.float32(REL_BUCKETS - max_exact))
    margin = np.abs(val - np.round(val))[max_exact + 1:]
    assert margin.min() > 1e-3, margin.min()
    large = np.minimum(max_exact + np.floor(val).astype(np.int64), REL_BUCKETS - 1)
    bucket = np.where(dist < max_exact, dist, large)
    assert np.all(np.diff(bucket) >= 0)
    return tuple(int(np.argmax(bucket >= b)) if np.any(bucket >= b) else WINDOW
                 for b in range(REL_BUCKETS))


T5_BUCKET_START = _t5_bucket_starts()

VMEM_LIMIT = 56 * 1024 * 1024

WIN_ROWS = 256
PROJ_TM = 512
MLA_TQ = 256
MLA_HEADS_PER_STEP = 2
SWA_BLOCKS_PER_STEP = 4
OUT_TM = 512
MLP_TM = 1024
MLP_TH = 1024
MLP_LAST_CHUNKS = 2
MLP_VMEM_LIMIT = 60 * 1024 * 1024

bf16 = jnp.bfloat16
f32 = jnp.float32


def _rms(x, g):
    return x * lax.rsqrt(jnp.mean(x * x, axis=-1, keepdims=True) + EPS) * g


def _dot(a, b):
    return jnp.dot(a, b, preferred_element_type=f32)


def _dot_nt(a, b):
    return lax.dot_general(a, b, (((1,), (1,)), ((), ())), preferred_element_type=f32)


def _dot_tn(a, b):
    return lax.dot_general(a, b, (((0,), (0,)), ((), ())), preferred_element_type=f32)


def _rope128(t, cos, sin):
    lane = lax.broadcasted_iota(jnp.int32, t.shape, 1)
    fwd = pltpu.roll(t, QK_ROPE // 2, axis=1)
    bwd = pltpu.roll(t, LANES - QK_ROPE // 2, axis=1)
    rot = jnp.where((lane & (QK_ROPE // 2)) == 0, -bwd, fwd)
    return t * cos + rot * sin


_SRC_KR = Q_LORA + KV_LORA
_SRC_QS = _SRC_KR + QK_ROPE
_SRC_KS = _SRC_QS + SWA_WIDTH
_SRC_END = _SRC_KS + 2 * SWA_KV_WIDTH


def _win_kernel(w_ref, wq_ref, wkv_ref, o_ref, wq_o_ref, wkv_o_ref):
    wkv_o_ref[...] = wkv_ref[...].astype(bf16)
    hd = QK_NOPE + QK_ROPE
    for h in range(MLA_HEADS):
        wq_o_ref[:, h * QK_PAD:h * QK_PAD + hd] = wq_ref[:, h * hd:(h + 1) * hd].astype(bf16)
        wq_o_ref[:, h * QK_PAD + hd:(h + 1) * QK_PAD] = jnp.zeros(
            (wq_o_ref.shape[0], QK_PAD - hd), bf16)
    o_ref[_C_Q:_C_QS, :] = w_ref[0:_SRC_KR, :].astype(bf16)
    o_ref[_C_QS:_C_KS, :] = w_ref[_SRC_QS:_SRC_KS, :].astype(bf16)
    o_ref[_C_KS:_C_KR, :] = w_ref[_SRC_KS:_SRC_END, :].astype(bf16)
    o_ref[_C_KR:_C_KR + QK_ROPE, :] = w_ref[_SRC_KR:_SRC_QS, :].astype(bf16)
    o_ref[_C_KR + QK_ROPE:IN_PAD, :] = jnp.zeros(
        (IN_PAD - _C_KR - QK_ROPE, o_ref.shape[1]), bf16)


def _proj_kernel(x_ref, pos_ref, inv_ref, gin_ref, win_ref, gq_ref, wq_ref, gkv_ref, wkv_ref,
                 wd_ref, q_ref, k_ref, v_ref, qs_ref, ks_ref, vst_ref, wd_o_ref):
    wd_o_ref[...] = wd_ref[...].astype(bf16)
    a = _rms(x_ref[...], gin_ref[...])
    proj = _dot_nt(a.astype(bf16), win_ref[...])

    ang = pos_ref[...].astype(f32) * inv_ref[...]
    cos = jnp.cos(ang)
    sin = jnp.sin(ang)

    tm = proj.shape[0]
    qs_ref[...] = (proj[:, _C_QS:_C_QS + SWA_WIDTH] * SWA_SCALE).astype(bf16)
    ks_ref[...] = proj[:, _C_KS:_C_KS + LANES].astype(bf16)
    vt = proj[:, _C_VS:_C_VS + LANES].T.astype(bf16)
    ones = jnp.ones((SWA_HEAD_DIM, tm), bf16)
    for kvh in range(SWA_KV_HEADS):
        v_t = vt[kvh * SWA_HEAD_DIM:(kvh + 1) * SWA_HEAD_DIM]
        vst_ref[kvh * LANES:(kvh + 1) * LANES, :] = jnp.concatenate([v_t, ones], axis=0)

    cq = _rms(proj[:, _C_Q:_C_Q + Q_LORA], gq_ref[...])
    qf = _dot(cq.astype(bf16), wq_ref[...])
    ckv = _rms(proj[:, _C_KV:_C_KV + KV_LORA], gkv_ref[...])
    kvf = _dot(ckv.astype(bf16), wkv_ref[...])
    kr = _rope128(proj[:, _C_KR:_C_KR + LANES], cos, sin).astype(bf16)
    for h in range(MLA_HEADS):
        c0 = h * QK_PAD
        q_ref[:, c0:c0 + LANES] = (qf[:, c0:c0 + LANES] * MLA_SCALE).astype(bf16)
        q_ref[:, c0 + LANES:c0 + QK_PAD] = (
            _rope128(qf[:, c0 + LANES:c0 + QK_PAD], cos, sin) * MLA_SCALE).astype(bf16)
        k_ref[:, c0:c0 + LANES] = kvf[:, c0:c0 + LANES].astype(bf16)
        k_ref[:, c0 + LANES:c0 + QK_PAD] = kr
        v_ref[:, h * V_HEAD:(h + 1) * V_HEAD] = kvf[:, c0 + LANES:c0 + QK_PAD].astype(bf16)


def _mla_kernel(q_ref, k_ref, v_ref, wu_ref, o_ref, wu_o_ref, s_ref, vt_ref):
    wu_o_ref[...] = wu_ref[...].astype(bf16)

    seq = q_ref.shape[0]
    tq = MLA_TQ
    n_tiles = seq // tq
    heads = range(MLA_HEADS_PER_STEP)
    order = list(reversed(range(n_tiles)))

    def scores(g, i):
        l = (i + 1) * tq
        s_ref[g, i % 2, 0:l, :] = _dot_nt(k_ref[0:l, g * QK_PAD:(g + 1) * QK_PAD],
                                          q_ref[i * tq:(i + 1) * tq, g * QK_PAD:(g + 1) * QK_PAD])

    for g in heads:
        vt_ref[g, 0:V_HEAD, :] = v_ref[:, g * V_HEAD:(g + 1) * V_HEAD].T
        vt_ref[g, V_HEAD:, :] = jnp.ones((VT_ROWS - V_HEAD, seq), bf16)
        scores(g, order[0])
    for step, i in enumerate(order):
        for g in heads:
            if step + 1 < n_tiles:
                scores(g, order[step + 1])
            l0 = i * tq
            key = lax.broadcasted_iota(jnp.int32, (tq, tq), 0)
            qry = lax.broadcasted_iota(jnp.int32, (tq, tq), 1)
            sd = jnp.where(key <= qry, s_ref[g, i % 2, l0:l0 + tq, :], -jnp.inf)
            m = jnp.max(sd, axis=0, keepdims=True)
            if i > 0:
                sa = s_ref[g, i % 2, 0:l0, :]
                m = jnp.maximum(m, jnp.max(sa, axis=0, keepdims=True))
            acc = _dot(vt_ref[g, :, l0:l0 + tq], jnp.exp2(sd - m).astype(bf16))
            if i > 0:
                acc = acc + _dot(vt_ref[g, :, 0:l0], jnp.exp2(sa - m).astype(bf16))
            o_ref[l0:l0 + tq, g * V_HEAD:(g + 1) * V_HEAD] = (
                acc[0:V_HEAD] / acc[V_HEAD:V_HEAD + 1]).T.astype(o_ref.dtype)


def _swa_bias_table(rel_ref, tab_ref):
    kj = lax.broadcasted_iota(jnp.int32, (2 * BLOCK, BLOCK), 0)
    qi = lax.broadcasted_iota(jnp.int32, (2 * BLOCK, BLOCK), 1) + BLOCK
    dist = qi - kj
    in_window = (dist >= 0) & (dist < WINDOW)
    for h in range(SWA_Q_HEADS):
        acc = jnp.full((2 * BLOCK, BLOCK), rel_ref[0, h] * LOG2E, f32)
        for b in range(1, REL_BUCKETS):
            if T5_BUCKET_START[b] < WINDOW:
                acc = jnp.where(dist >= T5_BUCKET_START[b], rel_ref[b, h] * LOG2E, acc)
        tab_ref[0, :, h * BLOCK:(h + 1) * BLOCK] = jnp.where(in_window, acc, -jnp.inf)
        tab_ref[1, :, h * BLOCK:(h + 1) * BLOCK] = jnp.where(
            in_window & (kj >= BLOCK), acc, -jnp.inf)


def _swa_kernel(rel_ref, sink_ref, q_ref, kp_ref, kc_ref, vp_ref, vc_ref, wo_ref,
                o_ref, wo_o_ref, tab_ref):
    wo_o_ref[...] = wo_ref[...].astype(bf16)
    first = (pl.program_id(0) == 0) & (pl.program_id(1) == 0)

    @pl.when(first)
    def _():
        _swa_bias_table(rel_ref, tab_ref)

    hd = SWA_HEAD_DIM
    width = SWA_GROUP * BLOCK
    zeros = jnp.zeros((hd, BLOCK), bf16)
    sink = jnp.concatenate(
        [jnp.full((1, BLOCK), sink_ref[0, h] * LOG2E, f32) for h in range(SWA_Q_HEADS)], axis=1)
    k_all = jnp.concatenate([kp_ref[...], kc_ref[...]], axis=0)
    v_all = jnp.concatenate([vp_ref[...], vc_ref[...]], axis=1)
    for j in range(SWA_BLOCKS_PER_STEP):
        rows = slice(j * BLOCK, (j + 1) * BLOCK)
        kband = k_all[j * BLOCK:(j + 2) * BLOCK]
        vband = v_all[:, j * BLOCK:(j + 2) * BLOCK]
        if j == 0:
            tab = tab_ref[(pl.program_id(1) == 0).astype(jnp.int32)]
        else:
            tab = tab_ref[0]
        q_t = q_ref[rows, :].T
        cols = []
        for h in range(SWA_Q_HEADS):
            qh = q_t[h * hd:(h + 1) * hd]
            cols.append(jnp.concatenate([qh, zeros] if h < SWA_GROUP else [zeros, qh], axis=0))
        s = _dot(kband, jnp.concatenate(cols, axis=1)) + tab
        m = jnp.maximum(jnp.max(s, axis=0, keepdims=True), sink)
        p = jnp.exp2(s - m).astype(bf16)
        sink_term = jnp.exp2(sink - m)
        outs = []
        for c in range(SWA_KV_HEADS):
            r = _dot(vband[c * LANES:(c + 1) * LANES, :], p[:, c * width:(c + 1) * width])
            outs.append(r[0:hd] / (r[hd:hd + 1] + sink_term[:, c * width:(c + 1) * width]))
        for pair in range(SWA_Q_HEADS // 2):
            halves = []
            for h in (2 * pair, 2 * pair + 1):
                g = h % SWA_GROUP
                halves.append(outs[h // SWA_GROUP][:, g * BLOCK:(g + 1) * BLOCK])
            o_ref[rows, pair * LANES:(pair + 1) * LANES] = (
                jnp.concatenate(halves, axis=0).T.astype(o_ref.dtype))


def _out_kernel(x_ref, ya_ref, yb_ref, wa_ref, wb_ref, g_ref, h_ref, m_ref):
    h = x_ref[...] + _dot(ya_ref[...], wa_ref[...]) + _dot(yb_ref[...], wb_ref[...])
    h_ref[...] = h
    m_ref[...] = _rms(h, g_ref[...]).astype(m_ref.dtype)


def _mlp_kernel(m_ref, h_hbm, wu_hbm, wd_hbm, g_ref, o_ref, hbuf, wubuf, wdbuf, hsem, wsem):
    i = pl.program_id(0)
    n_tiles = pl.num_programs(0)
    tm = o_ref.shape[0]
    th = wubuf.shape[2]
    n_chunks = wu_hbm.shape[1] // th
    assert n_chunks % 2 == 0

    def w_copies(j, slot):
        return (pltpu.make_async_copy(wu_hbm.at[:, pl.ds(j * th, th)], wubuf.at[slot], wsem.at[0, slot]),
                pltpu.make_async_copy(wd_hbm.at[pl.ds(j * th, th), :], wdbuf.at[slot], wsem.at[1, slot]))

    def start(copies):
        for c in copies:
            c.start()

    def wait(copies):
        for c in copies:
            c.wait()

    h_copy = pltpu.make_async_copy(h_hbm.at[pl.ds(i * tm, tm), :], hbuf, hsem)
    h_copy.start()

    @pl.when(i == 0)
    def _():
        start(w_copies(0, 0))

    for j in range(n_chunks):
        slot = j % 2
        if j + 1 < n_chunks:
            start(w_copies(j + 1, 1 - slot))
        else:
            @pl.when(i + 1 < n_tiles)
            def _():
                start(w_copies(0, 1 - slot))
        wait(w_copies(j, slot))
        last = j == n_chunks - 1
        if last:
            h_copy.wait()
        parts = MLP_LAST_CHUNKS if last else 1
        rows = tm // parts
        for c in range(parts):
            r = pl.ds(c * rows, rows)
            u = jnp.maximum(_dot(m_ref[r, :], wubuf[slot]), 0.0)
            acc = _dot((u * u).astype(bf16), wdbuf[slot])
            if j > 0:
                acc = o_ref[r, :] + acc
            if last:
                acc = _rms(acc + hbuf[r, :], g_ref[...])
            o_ref[r, :] = acc


def _params(sem):
    return pltpu.CompilerParams(dimension_semantics=sem, vmem_limit_bytes=VMEM_LIMIT)


def kernel(x, positions, rel_bias, attn_norm, w_in, q_a_norm, w_q_b, kv_a_norm, w_kv_b, sinks,
           w_out, mlp_norm, w_up, w_down, final_norm):
    batch, seq, d = x.shape
    tokens = batch * seq
    nb = seq // BLOCK
    depth = w_in.shape[0]
    assert depth == 1 and d == D_MODEL and seq % MLA_TQ == 0

    assert w_in.shape[2] == _SRC_END
    n_prep = d // WIN_ROWS
    lat_rows = Q_LORA // n_prep
    assert KV_LORA == Q_LORA
    slab3 = lambda rows, cols: pl.BlockSpec((None, rows, cols), lambda i: (0, i, 0))
    slab2 = lambda rows, cols: pl.BlockSpec((rows, cols), lambda i: (i, 0))
    kv_cols = MLA_HEADS * (QK_NOPE + V_HEAD)
    win, wq, wkv = pl.pallas_call(
        _win_kernel,
        grid=(n_prep,),
        in_specs=[pl.BlockSpec((_SRC_END, WIN_ROWS), lambda i: (0, i)),
                  slab3(lat_rows, MLA_HEADS * (QK_NOPE + QK_ROPE)), slab3(lat_rows, kv_cols)],
        out_specs=[pl.BlockSpec((IN_PAD, WIN_ROWS), lambda i: (0, i)),
                   slab2(lat_rows, MLA_HEADS * QK_PAD), slab2(lat_rows, kv_cols)],
        out_shape=[jax.ShapeDtypeStruct((IN_PAD, d), bf16),
                   jax.ShapeDtypeStruct((Q_LORA, MLA_HEADS * QK_PAD), bf16),
                   jax.ShapeDtypeStruct((KV_LORA, kv_cols), bf16)],
        compiler_params=_params(("arbitrary",)),
        name="winprep",
    )(jnp.transpose(w_in[0]), w_q_b, w_kv_b)

    inv = 1.0 / (ROPE_THETA ** (jnp.arange(0, QK_ROPE, 2, dtype=f32) / QK_ROPE))
    inv128 = jnp.concatenate([inv, inv, jnp.zeros((LANES - QK_ROPE,), f32)])[None, :]

    x2 = x.reshape(tokens, d)
    pos2 = positions.reshape(tokens, 1)
    row = lambda v: v.reshape(1, -1)

    tm = PROJ_TM
    full = lambda shape: pl.BlockSpec(shape, lambda i: (0, 0))
    tok = lambda w: pl.BlockSpec((tm, w), lambda i: (i, 0))
    vt_rows = SWA_KV_HEADS * LANES
    wd_slab = pl.BlockSpec((MLP_HIDDEN // (tokens // tm), d), lambda i: (i, 0))
    q, k, v, qs, ks, vst, wd = pl.pallas_call(
        _proj_kernel,
        grid=(tokens // tm,),
        in_specs=[tok(d), tok(1), full((1, LANES)), full((1, d)), full((IN_PAD, d)),
                  full((1, Q_LORA)), full((Q_LORA, MLA_HEADS * QK_PAD)),
                  full((1, KV_LORA)), full((KV_LORA, MLA_HEADS * (QK_NOPE + V_HEAD))),
                  wd_slab],
        out_specs=[tok(MLA_HEADS * QK_PAD), tok(MLA_HEADS * QK_PAD), tok(MLA_WIDTH),
                   tok(SWA_WIDTH), tok(SWA_KV_WIDTH),
                   pl.BlockSpec((vt_rows, tm), lambda i: (0, i)), wd_slab],
        out_shape=[jax.ShapeDtypeStruct((tokens, MLA_HEADS * QK_PAD), bf16),
                   jax.ShapeDtypeStruct((tokens, MLA_HEADS * QK_PAD), bf16),
                   jax.ShapeDtypeStruct((tokens, MLA_WIDTH), bf16),
                   jax.ShapeDtypeStruct((tokens, SWA_WIDTH), bf16),
                   jax.ShapeDtypeStruct((tokens, SWA_KV_WIDTH), bf16),
                   jax.ShapeDtypeStruct((vt_rows, tokens), bf16),
                   jax.ShapeDtypeStruct((MLP_HIDDEN, d), bf16)],
        compiler_params=_params(("arbitrary",)),
        name="proj",
    )(x2, pos2, inv128, row(attn_norm[0]), win, row(q_a_norm[0]), wq, row(kv_a_norm[0]), wkv,
      w_down[0])

    hps = MLA_HEADS_PER_STEP
    groups = MLA_HEADS // hps
    mla_steps = batch * groups
    wu_slab = pl.BlockSpec((d // mla_steps, MLP_HIDDEN), lambda b, h: (b * groups + h, 0))
    y_mla, wu = pl.pallas_call(
        _mla_kernel,
        grid=(batch, groups),
        in_specs=[pl.BlockSpec((seq, hps * QK_PAD), lambda b, h: (b, h)),
                  pl.BlockSpec((seq, hps * QK_PAD), lambda b, h: (b, h)),
                  pl.BlockSpec((seq, hps * V_HEAD), lambda b, h: (b, h)),
                  wu_slab],
        out_specs=[pl.BlockSpec((seq, hps * V_HEAD), lambda b, h: (b, h)), wu_slab],
        out_shape=[jax.ShapeDtypeStruct((tokens, MLA_WIDTH), bf16),
                   jax.ShapeDtypeStruct((d, MLP_HIDDEN), bf16)],
        scratch_shapes=[pltpu.VMEM((hps, 2, seq, MLA_TQ), f32),
                        pltpu.VMEM((hps, VT_ROWS, seq), bf16)],
        compiler_params=_params(("arbitrary", "arbitrary")),
        name="mla",
    )(q, k, v, w_up[0])

    smem = pl.BlockSpec(memory_space=pltpu.SMEM)
    bps = SWA_BLOCKS_PER_STEP
    ns = nb // bps
    rows = bps * BLOCK
    wo_rows = (MLA_WIDTH + SWA_WIDTH) // (batch * ns)
    wo_slab = pl.BlockSpec((wo_rows, d), lambda b, n: (b * ns + n, 0))
    prev = lambda b, n: b * nb + jnp.maximum(n * bps - 1, 0)
    y_swa, wo = pl.pallas_call(
        _swa_kernel,
        grid=(batch, ns),
        in_specs=[smem, smem,
                  pl.BlockSpec((rows, SWA_WIDTH), lambda b, n: (b * ns + n, 0)),
                  pl.BlockSpec((BLOCK, SWA_KV_WIDTH), lambda b, n: (prev(b, n), 0)),
                  pl.BlockSpec((rows, SWA_KV_WIDTH), lambda b, n: (b * ns + n, 0)),
                  pl.BlockSpec((vt_rows, BLOCK), lambda b, n: (0, prev(b, n))),
                  pl.BlockSpec((vt_rows, rows), lambda b, n: (0, b * ns + n)),
                  wo_slab],
        out_specs=[pl.BlockSpec((rows, SWA_WIDTH), lambda b, n: (b * ns + n, 0)), wo_slab],
        out_shape=[jax.ShapeDtypeStruct((tokens, SWA_WIDTH), bf16),
                   jax.ShapeDtypeStruct((MLA_WIDTH + SWA_WIDTH, d), bf16)],
        scratch_shapes=[pltpu.VMEM((2, 2 * BLOCK, SWA_Q_HEADS * BLOCK), f32)],
        compiler_params=_params(("arbitrary", "arbitrary")),
        name="swa",
    )(rel_bias, row(sinks[0]), qs, ks, ks, vst, vst, w_out[0])

    tm = OUT_TM
    h1, m = pl.pallas_call(
        _out_kernel,
        grid=(tokens // tm,),
        in_specs=[tok(d), tok(MLA_WIDTH), tok(SWA_WIDTH),
                  pl.BlockSpec((MLA_WIDTH, d), lambda i: (0, 0)),
                  pl.BlockSpec((SWA_WIDTH, d), lambda i: (1, 0)), full((1, d))],
        out_specs=[tok(d), tok(d)],
        out_shape=[jax.ShapeDtypeStruct((tokens, d), f32),
                   jax.ShapeDtypeStruct((tokens, d), bf16)],
        compiler_params=_params(("arbitrary",)),
        name="outproj",
    )(x2, y_mla, y_swa, wo, wo, row(mlp_norm[0]))

    tm, th = MLP_TM, MLP_TH
    hbm = pl.BlockSpec(memory_space=pl.ANY)
    out = pl.pallas_call(
        _mlp_kernel,
        grid=(tokens // tm,),
        in_specs=[pl.BlockSpec((tm, d), lambda i: (i, 0)), hbm, hbm, hbm,
                  pl.BlockSpec((1, d), lambda i: (0, 0))],
        out_specs=pl.BlockSpec((tm, d), lambda i: (i, 0)),
        out_shape=jax.ShapeDtypeStruct((tokens, d), f32),
        scratch_shapes=[pltpu.VMEM((tm, d), f32),
                        pltpu.VMEM((2, d, th), bf16), pltpu.VMEM((2, th, d), bf16),
                        pltpu.SemaphoreType.DMA(()), pltpu.SemaphoreType.DMA((2, 2))],
        compiler_params=pltpu.CompilerParams(
            dimension_semantics=("arbitrary",), vmem_limit_bytes=MLP_VMEM_LIMIT),
        name="mlp",
    )(m, h1, wu, wd, row(final_norm))

    return out.reshape(batch, seq, d)
```

```python
import jax
import jax.numpy as jnp
import numpy as np
from jax import lax
from jax.experimental import pallas as pl
from jax.experimental.pallas import tpu as pltpu

D_MODEL = 2048
MLA_HEADS = 8
QK_NOPE = 128
QK_ROPE = 64
V_HEAD = 128
Q_LORA = 512
KV_LORA = 512
ROPE_THETA = 10000.0
SWA_Q_HEADS = 16
SWA_KV_HEADS = 2
SWA_GROUP = SWA_Q_HEADS // SWA_KV_HEADS
SWA_HEAD_DIM = 64
WINDOW = 128
BLOCK = 128
REL_BUCKETS = 32
REL_MAX_DIST = 128
MLP_HIDDEN = 4 * D_MODEL
EPS = 1e-6
MLA_WIDTH = MLA_HEADS * V_HEAD
SWA_WIDTH = SWA_Q_HEADS * SWA_HEAD_DIM
SWA_KV_WIDTH = SWA_KV_HEADS * SWA_HEAD_DIM

LANES = 128
QK_PAD = 2 * LANES
VT_ROWS = V_HEAD + 16
LOG2E = float(np.log2(np.e))
MLA_SCALE = (QK_NOPE + QK_ROPE) ** -0.5 * LOG2E
SWA_SCALE = SWA_HEAD_DIM ** -0.5 * LOG2E

_C_Q = 0
_C_KV = _C_Q + Q_LORA
_C_QS = _C_KV + KV_LORA
_C_KS = _C_QS + SWA_WIDTH
_C_VS = _C_KS + SWA_KV_WIDTH
_C_KR = _C_VS + SWA_KV_WIDTH
IN_PAD = _C_KR + LANES

def _t5_bucket_starts():
    max_exact = REL_BUCKETS // 2
    dist = np.arange(WINDOW)
    val = (np.log(np.maximum(dist, 1).astype(np.float32) / np.float32(max_exact))
           / np.float32(np.log(REL_MAX_DIST / max_exact)) * np.float32(REL_BUCKETS - max_exact))
    margin = np.abs(val - np.round(val))[max_exact + 1:]
    assert margin.min() > 1e-3, margin.min()
    large = np.minimum(max_exact + np.floor(val).astype(np.int64), REL_BUCKETS - 1)
    bucket = np.where(dist < max_exact, dist, large)
    assert np.all(np.diff(bucket) >= 0)
    return tuple(int(np.argmax(bucket >= b)) if np.any(bucket >= b) else WINDOW
                 for b in range(REL_BUCKETS))


T5_BUCKET_START = _t5_bucket_starts()

VMEM_LIMIT = 56 * 1024 * 1024

WIN_ROWS = 256
PROJ_TM = 512
MLA_TQ = 256
MLA_HEADS_PER_STEP = 2
SWA_BLOCKS_PER_STEP = 4
OUT_TM = 512
MLP_TM = 1024
MLP_TH = 1024
MLP_LAST_CHUNKS = 2
MLP_VMEM_LIMIT = 60 * 1024 * 1024

bf16 = jnp.bfloat16
f32 = jnp.float32


def _rms(x, g):
    return x * lax.rsqrt(jnp.mean(x * x, axis=-1, keepdims=True) + EPS) * g


def _dot(a, b):
    return jnp.dot(a, b, preferred_element_type=f32)


def _dot_nt(a, b):
    return lax.dot_general(a, b, (((1,), (1,)), ((), ())), preferred_element_type=f32)


def _dot_tn(a, b):
    return lax.dot_general(a, b, (((0,), (0,)), ((), ())), preferred_element_type=f32)


def _rope128(t, cos, sin):
    lane = lax.broadcasted_iota(jnp.int32, t.shape, 1)
    fwd = pltpu.roll(t, QK_ROPE // 2, axis=1)
    bwd = pltpu.roll(t, LANES - QK_ROPE // 2, axis=1)
    rot = jnp.where((lane & (QK_ROPE // 2)) == 0, -bwd, fwd)
    return t * cos + rot * sin


_SRC_KR = Q_LORA + KV_LORA
_SRC_QS = _SRC_KR + QK_ROPE
_SRC_KS = _SRC_QS + SWA_WIDTH
_SRC_END = _SRC_KS + 2 * SWA_KV_WIDTH


def _win_kernel(w_ref, wq_ref, wkv_ref, o_ref, wq_o_ref, wkv_o_ref):
    wkv_o_ref[...] = wkv_ref[...].astype(bf16)
    hd = QK_NOPE + QK_ROPE
    for h in range(MLA_HEADS):
        wq_o_ref[:, h * QK_PAD:h * QK_PAD + hd] = wq_ref[:, h * hd:(h + 1) * hd].astype(bf16)
        wq_o_ref[:, h * QK_PAD + hd:(h + 1) * QK_PAD] = jnp.zeros(
            (wq_o_ref.shape[0], QK_PAD - hd), bf16)
    o_ref[_C_Q:_C_QS, :] = w_ref[0:_SRC_KR, :].astype(bf16)
    o_ref[_C_QS:_C_KS, :] = w_ref[_SRC_QS:_SRC_KS, :].astype(bf16)
    o_ref[_C_KS:_C_KR, :] = w_ref[_SRC_KS:_SRC_END, :].astype(bf16)
    o_ref[_C_KR:_C_KR + QK_ROPE, :] = w_ref[_SRC_KR:_SRC_QS, :].astype(bf16)
    o_ref[_C_KR + QK_ROPE:IN_PAD, :] = jnp.zeros(
        (IN_PAD - _C_KR - QK_ROPE, o_ref.shape[1]), bf16)


def _proj_kernel(x_ref, pos_ref, inv_ref, gin_ref, win_ref, gq_ref, wq_ref, gkv_ref, wkv_ref,
                 wd_ref, q_ref, k_ref, v_ref, qs_ref, ks_ref, vst_ref, wd_o_ref):
    wd_o_ref[...] = wd_ref[...].astype(bf16)
    a = _rms(x_ref[...], gin_ref[...])
    proj = _dot_nt(a.astype(bf16), win_ref[...])

    ang = pos_ref[...].astype(f32) * inv_ref[...]
    cos = jnp.cos(ang)
    sin = jnp.sin(ang)

    tm = proj.shape[0]
    qs_ref[...] = (proj[:, _C_QS:_C_QS + SWA_WIDTH] * SWA_SCALE).astype(bf16)
    ks_ref[...] = proj[:, _C_KS:_C_KS + LANES].astype(bf16)
    vt = proj[:, _C_VS:_C_VS + LANES].T.astype(bf16)
    ones = jnp.ones((SWA_HEAD_DIM, tm), bf16)
    for kvh in range(SWA_KV_HEADS):
        v_t = vt[kvh * SWA_HEAD_DIM:(kvh + 1) * SWA_HEAD_DIM]
        vst_ref[kvh * LANES:(kvh + 1) * LANES, :] = jnp.concatenate([v_t, ones], axis=0)

    cq = _rms(proj[:, _C_Q:_C_Q + Q_LORA], gq_ref[...])
    qf = _dot(cq.astype(bf16), wq_ref[...])
    ckv = _rms(proj[:, _C_KV:_C_KV + KV_LORA], gkv_ref[...])
    kvf = _dot(ckv.astype(bf16), wkv_ref[...])
    kr = _rope128(proj[:, _C_KR:_C_KR + LANES], cos, sin).astype(bf16)
    for h in range(MLA_HEADS):
        c0 = h * QK_PAD
        q_ref[:, c0:c0 + LANES] = (qf[:, c0:c0 + LANES] * MLA_SCALE).astype(bf16)
        q_ref[:, c0 + LANES:c0 + QK_PAD] = (
            _rope128(qf[:, c0 + LANES:c0 + QK_PAD], cos, sin) * MLA_SCALE).astype(bf16)
        k_ref[:, c0:c0 + LANES] = kvf[:, c0:c0 + LANES].astype(bf16)
        k_ref[:, c0 + LANES:c0 + QK_PAD] = kr
        v_ref[:, h * V_HEAD:(h + 1) * V_HEAD] = kvf[:, c0 + LANES:c0 + QK_PAD].astype(bf16)


def _mla_kernel(q_ref, k_ref, v_ref, wu_ref, o_ref, wu_o_ref, s_ref, vt_ref):
    wu_o_ref[...] = wu_ref[...].astype(bf16)

    seq = q_ref.shape[0]
    tq = MLA_TQ
    n_tiles = seq // tq
    heads = range(MLA_HEADS_PER_STEP)
    order = list(reversed(range(n_tiles)))

    def scores(g, i):
        l = (i + 1) * tq
        s_ref[g, i % 2, 0:l, :] = _dot_nt(k_ref[0:l, g * QK_PAD:(g + 1) * QK_PAD],
                                          q_ref[i * tq:(i + 1) * tq, g * QK_PAD:(g + 1) * QK_PAD])

    for g in heads:
        vt_ref[g, 0:V_HEAD, :] = v_ref[:, g * V_HEAD:(g + 1) * V_HEAD].T
        vt_ref[g, V_HEAD:, :] = jnp.ones((VT_ROWS - V_HEAD, seq), bf16)
        scores(g, order[0])
    for step, i in enumerate(order):
        for g in heads:
            if step + 1 < n_tiles:
                scores(g, order[step + 1])
            l0 = i * tq
            key = lax.broadcasted_iota(jnp.int32, (tq, tq), 0)
            qry = lax.broadcasted_iota(jnp.int32, (tq, tq), 1)
            sd = jnp.where(key <= qry, s_ref[g, i % 2, l0:l0 + tq, :], -jnp.inf)
            m = jnp.max(sd, axis=0, keepdims=True)
            if i > 0:
                sa = s_ref[g, i % 2, 0:l0, :]
                m = jnp.maximum(m, jnp.max(sa, axis=0, keepdims=True))
            acc = _dot(vt_ref[g, :, l0:l0 + tq], jnp.exp2(sd - m).astype(bf16))
            if i > 0:
                acc = acc + _dot(vt_ref[g, :, 0:l0], jnp.exp2(sa - m).astype(bf16))
            o_ref[l0:l0 + tq, g * V_HEAD:(g + 1) * V_HEAD] = (
                acc[0:V_HEAD] / acc[V_HEAD:V_HEAD + 1]).T.astype(o_ref.dtype)


def _swa_bias_table(rel_ref, tab_ref):
    kj = lax.broadcasted_iota(jnp.int32, (2 * BLOCK, BLOCK), 0)
    qi = lax.broadcasted_iota(jnp.int32, (2 * BLOCK, BLOCK), 1) + BLOCK
    dist = qi - kj
    in_window = (dist >= 0) & (dist < WINDOW)
    for h in range(SWA_Q_HEADS):
        acc = jnp.full((2 * BLOCK, BLOCK), rel_ref[0, h] * LOG2E, f32)
        for b in range(1, REL_BUCKETS):
            if T5_BUCKET_START[b] < WINDOW:
                acc = jnp.where(dist >= T5_BUCKET_START[b], rel_ref[b, h] * LOG2E, acc)
        tab_ref[0, :, h * BLOCK:(h + 1) * BLOCK] = jnp.where(in_window, acc, -jnp.inf)
        tab_ref[1, :, h * BLOCK:(h + 1) * BLOCK] = jnp.where(
            in_window & (kj >= BLOCK), acc, -jnp.inf)


def _swa_kernel(rel_ref, sink_ref, q_ref, kp_ref, kc_ref, vp_ref, vc_ref, wo_ref,
                o_ref, wo_o_ref, tab_ref):
    wo_o_ref[...] = wo_ref[...].astype(bf16)
    hd = SWA_HEAD_DIM
    width = SWA_GROUP * BLOCK
    zeros = jnp.zeros((hd, BLOCK), bf16)
    sink = jnp.concatenate(
        [jnp.full((1, BLOCK), sink_ref[0, h] * LOG2E, f32) for h in range(SWA_Q_HEADS)], axis=1)
    k_all = jnp.concatenate([kp_ref[...], kc_ref[...]], axis=0)
    v_all = jnp.concatenate([vp_ref[...], vc_ref[...]], axis=1)
    for j in range(SWA_BLOCKS_PER_STEP):
        rows = slice(j * BLOCK, (j + 1) * BLOCK)
        kband = k_all[j * BLOCK:(j + 2) * BLOCK]
        vband = v_all[:, j * BLOCK:(j + 2) * BLOCK]
        if j == 0:
            tab = tab_ref[(pl.program_id(1) == 0).astype(jnp.int32)]
        else:
            tab = tab_ref[0]
        q_t = q_ref[rows, :].T
        cols = []
        for h in range(SWA_Q_HEADS):
            qh = q_t[h * hd:(h + 1) * hd]
            cols.append(jnp.concatenate([qh, zeros] if h < SWA_GROUP else [zeros, qh], axis=0))
        s = _dot(kband, jnp.concatenate(cols, axis=1)) + tab
        m = jnp.maximum(jnp.max(s, axis=0, keepdims=True), sink)
        p = jnp.exp2(s - m).astype(bf16)
        sink_term = jnp.exp2(sink - m)
        outs = []
        for c in range(SWA_KV_HEADS):
            r = _dot(vband[c * LANES:(c + 1) * LANES, :], p[:, c * width:(c + 1) * width])
            outs.append(r[0:hd] / (r[hd:hd + 1] + sink_term[:, c * width:(c + 1) * width]))
        for pair in range(SWA_Q_HEADS // 2):
            halves = []
            for h in (2 * pair, 2 * pair + 1):
                g = h % SWA_GROUP
                halves.append(outs[h // SWA_GROUP][:, g * BLOCK:(g + 1) * BLOCK])
            o_ref[rows, pair * LANES:(pair + 1) * LANES] = (
                jnp.concatenate(halves, axis=0).T.astype(o_ref.dtype))


def _attn_kernel(q_ref, k_ref, v_ref, wu_ref, rel_ref, sink_ref, qs_ref, kp_ref, kc_ref, vp_ref,
                 vc_ref, wo_ref, o_ref, wu_o_ref, os_ref, wo_o_ref, s_ref, vt_ref, tab_ref):
    @pl.when((pl.program_id(0) == 0) & (pl.program_id(1) == 0))
    def _():
        _swa_bias_table(rel_ref, tab_ref)

    _mla_kernel(q_ref, k_ref, v_ref, wu_ref, o_ref, wu_o_ref, s_ref, vt_ref)
    _swa_kernel(rel_ref, sink_ref, qs_ref, kp_ref, kc_ref, vp_ref, vc_ref, wo_ref,
                os_ref, wo_o_ref, tab_ref)


def _out_kernel(x_ref, ya_ref, yb_ref, wa_ref, wb_ref, g_ref, h_ref, m_ref):
    h = x_ref[...] + _dot(ya_ref[...], wa_ref[...]) + _dot(yb_ref[...], wb_ref[...])
    h_ref[...] = h
    m_ref[...] = _rms(h, g_ref[...]).astype(m_ref.dtype)


def _mlp_kernel(m_ref, h_hbm, wu_ref, wd_ref, g_ref, o_ref, hbuf, sem):
    i = pl.program_id(0)
    j = pl.program_id(1)
    last_j = pl.num_programs(1) - 1
    tm = o_ref.shape[0]
    h_copy = pltpu.make_async_copy(h_hbm.at[pl.ds(i * tm, tm), :], hbuf, sem)

    def step(first, last):
        n_chunks = MLP_LAST_CHUNKS if last else 1
        rows = tm // n_chunks
        for c in range(n_chunks):
            r = pl.ds(c * rows, rows)
            u = jnp.maximum(_dot(m_ref[r, :], wu_ref[...]), 0.0)
            acc = _dot((u * u).astype(bf16), wd_ref[...])
            if not first:
                acc = o_ref[r, :] + acc
            if last:
                acc = _rms(acc + hbuf[r, :], g_ref[...])
            o_ref[r, :] = acc

    @pl.when(j == 0)
    def _():
        h_copy.start()
        step(first=True, last=False)

    @pl.when((j > 0) & (j < last_j))
    def _():
        step(first=False, last=False)

    @pl.when(j == last_j)
    def _():
        h_copy.wait()
        step(first=False, last=True)


def _params(sem):
    return pltpu.CompilerParams(dimension_semantics=sem, vmem_limit_bytes=VMEM_LIMIT)


def kernel(x, positions, rel_bias, attn_norm, w_in, q_a_norm, w_q_b, kv_a_norm, w_kv_b, sinks,
           w_out, mlp_norm, w_up, w_down, final_norm):
    batch, seq, d = x.shape
    tokens = batch * seq
    nb = seq // BLOCK
    depth = w_in.shape[0]
    assert depth == 1 and d == D_MODEL and seq % MLA_TQ == 0

    assert w_in.shape[2] == _SRC_END
    n_prep = d // WIN_ROWS
    lat_rows = Q_LORA // n_prep
    assert KV_LORA == Q_LORA
    slab3 = lambda rows, cols: pl.BlockSpec((None, rows, cols), lambda i: (0, i, 0))
    slab2 = lambda rows, cols: pl.BlockSpec((rows, cols), lambda i: (i, 0))
    kv_cols = MLA_HEADS * (QK_NOPE + V_HEAD)
    win, wq, wkv = pl.pallas_call(
        _win_kernel,
        grid=(n_prep,),
        in_specs=[pl.BlockSpec((_SRC_END, WIN_ROWS), lambda i: (0, i)),
                  slab3(lat_rows, MLA_HEADS * (QK_NOPE + QK_ROPE)), slab3(lat_rows, kv_cols)],
        out_specs=[pl.BlockSpec((IN_PAD, WIN_ROWS), lambda i: (0, i)),
                   slab2(lat_rows, MLA_HEADS * QK_PAD), slab2(lat_rows, kv_cols)],
        out_shape=[jax.ShapeDtypeStruct((IN_PAD, d), bf16),
                   jax.ShapeDtypeStruct((Q_LORA, MLA_HEADS * QK_PAD), bf16),
                   jax.ShapeDtypeStruct((KV_LORA, kv_cols), bf16)],
        compiler_params=_params(("arbitrary",)),
        name="winprep",
    )(jnp.transpose(w_in[0]), w_q_b, w_kv_b)

    inv = 1.0 / (ROPE_THETA ** (jnp.arange(0, QK_ROPE, 2, dtype=f32) / QK_ROPE))
    inv128 = jnp.concatenate([inv, inv, jnp.zeros((LANES - QK_ROPE,), f32)])[None, :]

    x2 = x.reshape(tokens, d)
    pos2 = positions.reshape(tokens, 1)
    row = lambda v: v.reshape(1, -1)

    tm = PROJ_TM
    full = lambda shape: pl.BlockSpec(shape, lambda i: (0, 0))
    tok = lambda w: pl.BlockSpec((tm, w), lambda i: (i, 0))
    vt_rows = SWA_KV_HEADS * LANES
    wd_slab = pl.BlockSpec((MLP_HIDDEN // (tokens // tm), d), lambda i: (i, 0))
    q, k, v, qs, ks, vst, wd = pl.pallas_call(
        _proj_kernel,
        grid=(tokens // tm,),
        in_specs=[tok(d), tok(1), full((1, LANES)), full((1, d)), full((IN_PAD, d)),
                  full((1, Q_LORA)), full((Q_LORA, MLA_HEADS * QK_PAD)),
                  full((1, KV_LORA)), full((KV_LORA, MLA_HEADS * (QK_NOPE + V_HEAD))),
                  wd_slab],
        out_specs=[tok(MLA_HEADS * QK_PAD), tok(MLA_HEADS * QK_PAD), tok(MLA_WIDTH),
                   tok(SWA_WIDTH), tok(SWA_KV_WIDTH),
                   pl.BlockSpec((vt_rows, tm), lambda i: (0, i)), wd_slab],
        out_shape=[jax.ShapeDtypeStruct((tokens, MLA_HEADS * QK_PAD), bf16),
                   jax.ShapeDtypeStruct((tokens, MLA_HEADS * QK_PAD), bf16),
                   jax.ShapeDtypeStruct((tokens, MLA_WIDTH), bf16),
                   jax.ShapeDtypeStruct((tokens, SWA_WIDTH), bf16),
                   jax.ShapeDtypeStruct((tokens, SWA_KV_WIDTH), bf16),
                   jax.ShapeDtypeStruct((vt_rows, tokens), bf16),
                   jax.ShapeDtypeStruct((MLP_HIDDEN, d), bf16)],
        compiler_params=_params(("arbitrary",)),
        name="proj",
    )(x2, pos2, inv128, row(attn_norm[0]), win, row(q_a_norm[0]), wq, row(kv_a_norm[0]), wkv,
      w_down[0])

    hps = MLA_HEADS_PER_STEP
    bps = SWA_BLOCKS_PER_STEP
    steps = MLA_HEADS // hps
    assert nb // bps == steps
    rows = bps * BLOCK
    smem = pl.BlockSpec(memory_space=pltpu.SMEM)
    wu_slab = pl.BlockSpec((d // (batch * steps), MLP_HIDDEN), lambda b, t: (b * steps + t, 0))
    wo_slab = pl.BlockSpec(((MLA_WIDTH + SWA_WIDTH) // (batch * steps), d),
                           lambda b, t: (b * steps + t, 0))
    prev = lambda b, t: b * nb + jnp.maximum(t * bps - 1, 0)
    y_mla, wu, y_swa, wo = pl.pallas_call(
        _attn_kernel,
        grid=(batch, steps),
        in_specs=[pl.BlockSpec((seq, hps * QK_PAD), lambda b, t: (b, t)),
                  pl.BlockSpec((seq, hps * QK_PAD), lambda b, t: (b, t)),
                  pl.BlockSpec((seq, hps * V_HEAD), lambda b, t: (b, t)),
                  wu_slab,
                  smem, smem,
                  pl.BlockSpec((rows, SWA_WIDTH), lambda b, t: (b * steps + t, 0)),
                  pl.BlockSpec((BLOCK, SWA_KV_WIDTH), lambda b, t: (prev(b, t), 0)),
                  pl.BlockSpec((rows, SWA_KV_WIDTH), lambda b, t: (b * steps + t, 0)),
                  pl.BlockSpec((vt_rows, BLOCK), lambda b, t: (0, prev(b, t))),
                  pl.BlockSpec((vt_rows, rows), lambda b, t: (0, b * steps + t)),
                  wo_slab],
        out_specs=[pl.BlockSpec((seq, hps * V_HEAD), lambda b, t: (b, t)), wu_slab,
                   pl.BlockSpec((rows, SWA_WIDTH), lambda b, t: (b * steps + t, 0)), wo_slab],
        out_shape=[jax.ShapeDtypeStruct((tokens, MLA_WIDTH), bf16),
                   jax.ShapeDtypeStruct((d, MLP_HIDDEN), bf16),
                   jax.ShapeDtypeStruct((tokens, SWA_WIDTH), bf16),
                   jax.ShapeDtypeStruct((MLA_WIDTH + SWA_WIDTH, d), bf16)],
        scratch_shapes=[pltpu.VMEM((hps, 2, seq, MLA_TQ), f32),
                        pltpu.VMEM((hps, VT_ROWS, seq), bf16),
                        pltpu.VMEM((2, 2 * BLOCK, SWA_Q_HEADS * BLOCK), f32)],
        compiler_params=_params(("arbitrary", "arbitrary")),
        name="attn",
    )(q, k, v, w_up[0], rel_bias, row(sinks[0]), qs, ks, ks, vst, vst, w_out[0])

    tm = OUT_TM
    h1, m = pl.pallas_call(
        _out_kernel,
        grid=(tokens // tm,),
        in_specs=[tok(d), tok(MLA_WIDTH), tok(SWA_WIDTH),
                  pl.BlockSpec((MLA_WIDTH, d), lambda i: (0, 0)),
                  pl.BlockSpec((SWA_WIDTH, d), lambda i: (1, 0)), full((1, d))],
        out_specs=[tok(d), tok(d)],
        out_shape=[jax.ShapeDtypeStruct((tokens, d), f32),
                   jax.ShapeDtypeStruct((tokens, d), bf16)],
        compiler_params=_params(("arbitrary",)),
        name="outproj",
    )(x2, y_mla, y_swa, wo, wo, row(mlp_norm[0]))

    tm, th = MLP_TM, MLP_TH
    assert MLP_HIDDEN // th >= 2
    out = pl.pallas_call(
        _mlp_kernel,
        grid=(tokens // tm, MLP_HIDDEN // th),
        in_specs=[pl.BlockSpec((tm, d), lambda i, j: (i, 0)),
                  pl.BlockSpec(memory_space=pl.ANY),
                  pl.BlockSpec((d, th), lambda i, j: (0, j)),
                  pl.BlockSpec((th, d), lambda i, j: (j, 0)),
                  pl.BlockSpec((1, d), lambda i, j: (0, 0))],
        out_specs=pl.BlockSpec((tm, d), lambda i, j: (i, 0)),
        out_shape=jax.ShapeDtypeStruct((tokens, d), f32),
        scratch_shapes=[pltpu.VMEM((tm, d), f32), pltpu.SemaphoreType.DMA(())],
        compiler_params=pltpu.CompilerParams(
            dimension_semantics=("arbitrary", "arbitrary"), vmem_limit_bytes=MLP_VMEM_LIMIT),
        name="mlp",
    )(m, h1, wu, wd, row(final_norm))

    return out.reshape(batch, seq, d)
```

```python
import jax
import jax.numpy as jnp
import numpy as np
from jax import lax
from jax.experimental import pallas as pl
from jax.experimental.pallas import tpu as pltpu

D_MODEL = 2048
MLA_HEADS = 8
QK_NOPE = 128
QK_ROPE = 64
V_HEAD = 128
Q_LORA = 512
KV_LORA = 512
ROPE_THETA = 10000.0
SWA_Q_HEADS = 16
SWA_KV_HEADS = 2
SWA_GROUP = SWA_Q_HEADS // SWA_KV_HEADS
SWA_HEAD_DIM = 64
WINDOW = 128
BLOCK = 128
REL_BUCKETS = 32
REL_MAX_DIST = 128
MLP_HIDDEN = 4 * D_MODEL
EPS = 1e-6
MLA_WIDTH = MLA_HEADS * V_HEAD
SWA_WIDTH = SWA_Q_HEADS * SWA_HEAD_DIM
SWA_KV_WIDTH = SWA_KV_HEADS * SWA_HEAD_DIM

LANES = 128
QK_PAD = 2 * LANES
VT_ROWS = V_HEAD + 16
LOG2E = float(np.log2(np.e))
MLA_SCALE = (QK_NOPE + QK_ROPE) ** -0.5 * LOG2E
SWA_SCALE = SWA_HEAD_DIM ** -0.5 * LOG2E

_C_Q = 0
_C_KV = _C_Q + Q_LORA
_C_QS = _C_KV + KV_LORA
_C_KS = _C_QS + SWA_WIDTH
_C_VS = _C_KS + SWA_KV_WIDTH
_C_KR = _C_VS + SWA_KV_WIDTH
IN_PAD = _C_KR + LANES

def _t5_bucket_starts():
    max_exact = REL_BUCKETS // 2
    dist = np.arange(WINDOW)
    val = (np.log(np.maximum(dist, 1).astype(np.float32) / np.float32(max_exact))
           / np.float32(np.log(REL_MAX_DIST / max_exact)) * np.float32(REL_BUCKETS - max_exact))
    margin = np.abs(val - np.round(val))[max_exact + 1:]
    assert margin.min() > 1e-3, margin.min()
    large = np.minimum(max_exact + np.floor(val).astype(np.int64), REL_BUCKETS - 1)
    bucket = np.where(dist < max_exact, dist, large)
    assert np.all(np.diff(bucket) >= 0)
    return tuple(int(np.argmax(bucket >= b)) if np.any(bucket >= b) else WINDOW
                 for b in range(REL_BUCKETS))


T5_BUCKET_START = _t5_bucket_starts()

VMEM_LIMIT = 56 * 1024 * 1024

WIN_ROWS = 256
PROJ_TM = 512
MLA_TQ = 512
MLA_HEADS_PER_STEP = 2
SWA_BLOCKS_PER_STEP = 4
OUT_TM = 512
MLP_TM = 1024
MLP_TH = 1024
MLP_LAST_CHUNKS = 2
MLP_VMEM_LIMIT = 60 * 1024 * 1024

bf16 = jnp.bfloat16
f32 = jnp.float32


def _rms(x, g):
    return x * lax.rsqrt(jnp.mean(x * x, axis=-1, keepdims=True) + EPS) * g


def _dot(a, b):
    return jnp.dot(a, b, preferred_element_type=f32)


def _dot_nt(a, b):
    return lax.dot_general(a, b, (((1,), (1,)), ((), ())), preferred_element_type=f32)


def _dot_tn(a, b):
    return lax.dot_general(a, b, (((0,), (0,)), ((), ())), preferred_element_type=f32)


def _rope128(t, cos, sin):
    lane = lax.broadcasted_iota(jnp.int32, t.shape, 1)
    fwd = pltpu.roll(t, QK_ROPE // 2, axis=1)
    bwd = pltpu.roll(t, LANES - QK_ROPE // 2, axis=1)
    rot = jnp.where((lane & (QK_ROPE // 2)) == 0, -bwd, fwd)
    return t * cos + rot * sin


_SRC_KR = Q_LORA + KV_LORA
_SRC_QS = _SRC_KR + QK_ROPE
_SRC_KS = _SRC_QS + SWA_WIDTH
_SRC_END = _SRC_KS + 2 * SWA_KV_WIDTH


def _win_kernel(w_ref, wq_ref, wkv_ref, o_ref, wq_o_ref, wkv_o_ref):
    wkv_o_ref[...] = wkv_ref[...].astype(bf16)
    hd = QK_NOPE + QK_ROPE
    for h in range(MLA_HEADS):
        wq_o_ref[:, h * QK_PAD:h * QK_PAD + hd] = wq_ref[:, h * hd:(h + 1) * hd].astype(bf16)
        wq_o_ref[:, h * QK_PAD + hd:(h + 1) * QK_PAD] = jnp.zeros(
            (wq_o_ref.shape[0], QK_PAD - hd), bf16)
    o_ref[_C_Q:_C_QS, :] = w_ref[0:_SRC_KR, :].astype(bf16)
    o_ref[_C_QS:_C_KS, :] = w_ref[_SRC_QS:_SRC_KS, :].astype(bf16)
    o_ref[_C_KS:_C_KR, :] = w_ref[_SRC_KS:_SRC_END, :].astype(bf16)
    o_ref[_C_KR:_C_KR + QK_ROPE, :] = w_ref[_SRC_KR:_SRC_QS, :].astype(bf16)
    o_ref[_C_KR + QK_ROPE:IN_PAD, :] = jnp.zeros(
        (IN_PAD - _C_KR - QK_ROPE, o_ref.shape[1]), bf16)


def _proj_kernel(x_ref, pos_ref, inv_ref, gin_ref, win_ref, gq_ref, wq_ref, gkv_ref, wkv_ref,
                 wd_ref, q_ref, k_ref, v_ref, qs_ref, ks_ref, vst_ref, wd_o_ref):
    wd_o_ref[...] = wd_ref[...].astype(bf16)
    a = _rms(x_ref[...], gin_ref[...])
    proj = _dot_nt(a.astype(bf16), win_ref[...])

    ang = pos_ref[...].astype(f32) * inv_ref[...]
    cos = jnp.cos(ang)
    sin = jnp.sin(ang)

    tm = proj.shape[0]
    qs_ref[...] = (proj[:, _C_QS:_C_QS + SWA_WIDTH] * SWA_SCALE).astype(bf16)
    ks_ref[...] = proj[:, _C_KS:_C_KS + LANES].astype(bf16)
    vt = proj[:, _C_VS:_C_VS + LANES].T.astype(bf16)
    ones = jnp.ones((SWA_HEAD_DIM, tm), bf16)
    for kvh in range(SWA_KV_HEADS):
        v_t = vt[kvh * SWA_HEAD_DIM:(kvh + 1) * SWA_HEAD_DIM]
        vst_ref[kvh * LANES:(kvh + 1) * LANES, :] = jnp.concatenate([v_t, ones], axis=0)

    cq = _rms(proj[:, _C_Q:_C_Q + Q_LORA], gq_ref[...])
    qf = _dot(cq.astype(bf16), wq_ref[...])
    ckv = _rms(proj[:, _C_KV:_C_KV + KV_LORA], gkv_ref[...])
    kvf = _dot(ckv.astype(bf16), wkv_ref[...])
    kr = _rope128(proj[:, _C_KR:_C_KR + LANES], cos, sin).astype(bf16)
    for h in range(MLA_HEADS):
        c0 = h * QK_PAD
        q_ref[:, c0:c0 + LANES] = (qf[:, c0:c0 + LANES] * MLA_SCALE).astype(bf16)
        q_ref[:, c0 + LANES:c0 + QK_PAD] = (
            _rope128(qf[:, c0 + LANES:c0 + QK_PAD], cos, sin) * MLA_SCALE).astype(bf16)
        k_ref[:, c0:c0 + LANES] = kvf[:, c0:c0 + LANES].astype(bf16)
        k_ref[:, c0 + LANES:c0 + QK_PAD] = kr
        v_ref[:, h * V_HEAD:(h + 1) * V_HEAD] = kvf[:, c0 + LANES:c0 + QK_PAD].astype(bf16)


def _mla_kernel(q_ref, k_ref, v_ref, wu_ref, o_ref, wu_o_ref, s_ref, vt_ref):
    wu_o_ref[...] = wu_ref[...].astype(bf16)

    seq = q_ref.shape[0]
    tq = MLA_TQ
    n_tiles = seq // tq
    heads = range(MLA_HEADS_PER_STEP)
    order = list(reversed(range(n_tiles)))

    def scores(g, i):
        l = (i + 1) * tq
        s_ref[g, i % 2, 0:l, :] = _dot_nt(k_ref[0:l, g * QK_PAD:(g + 1) * QK_PAD],
                                          q_ref[i * tq:(i + 1) * tq, g * QK_PAD:(g + 1) * QK_PAD])

    for g in heads:
        vt_ref[g, 0:V_HEAD, :] = v_ref[:, g * V_HEAD:(g + 1) * V_HEAD].T
        vt_ref[g, V_HEAD:, :] = jnp.ones((VT_ROWS - V_HEAD, seq), bf16)
        scores(g, order[0])
    for step, i in enumerate(order):
        for g in heads:
            if step + 1 < n_tiles:
                scores(g, order[step + 1])
            l0 = i * tq
            key = lax.broadcasted_iota(jnp.int32, (tq, tq), 0)
            qry = lax.broadcasted_iota(jnp.int32, (tq, tq), 1)
            sd = jnp.where(key <= qry, s_ref[g, i % 2, l0:l0 + tq, :], -jnp.inf)
            m = jnp.max(sd, axis=0, keepdims=True)
            if i > 0:
                sa = s_ref[g, i % 2, 0:l0, :]
                m = jnp.maximum(m, jnp.max(sa, axis=0, keepdims=True))
            acc = _dot(vt_ref[g, :, l0:l0 + tq], jnp.exp2(sd - m).astype(bf16))
            if i > 0:
                acc = acc + _dot(vt_ref[g, :, 0:l0], jnp.exp2(sa - m).astype(bf16))
            o_ref[l0:l0 + tq, g * V_HEAD:(g + 1) * V_HEAD] = (
                acc[0:V_HEAD] / acc[V_HEAD:V_HEAD + 1]).T.astype(o_ref.dtype)


def _swa_bias_table(rel_ref, tab_ref):
    kj = lax.broadcasted_iota(jnp.int32, (2 * BLOCK, BLOCK), 0)
    qi = lax.broadcasted_iota(jnp.int32, (2 * BLOCK, BLOCK), 1) + BLOCK
    dist = qi - kj
    in_window = (dist >= 0) & (dist < WINDOW)
    for h in range(SWA_Q_HEADS):
        acc = jnp.full((2 * BLOCK, BLOCK), rel_ref[0, h] * LOG2E, f32)
        for b in range(1, REL_BUCKETS):
            if T5_BUCKET_START[b] < WINDOW:
                acc = jnp.where(dist >= T5_BUCKET_START[b], rel_ref[b, h] * LOG2E, acc)
        tab_ref[0, :, h * BLOCK:(h + 1) * BLOCK] = jnp.where(in_window, acc, -jnp.inf)
        tab_ref[1, :, h * BLOCK:(h + 1) * BLOCK] = jnp.where(
            in_window & (kj >= BLOCK), acc, -jnp.inf)


def _swa_kernel(rel_ref, sink_ref, q_ref, kp_ref, kc_ref, vp_ref, vc_ref, wo_ref,
                o_ref, wo_o_ref, tab_ref):
    wo_o_ref[...] = wo_ref[...].astype(bf16)
    hd = SWA_HEAD_DIM
    width = SWA_GROUP * BLOCK
    zeros = jnp.zeros((hd, BLOCK), bf16)
    sink = jnp.concatenate(
        [jnp.full((1, BLOCK), sink_ref[0, h] * LOG2E, f32) for h in range(SWA_Q_HEADS)], axis=1)
    k_all = jnp.concatenate([kp_ref[...], kc_ref[...]], axis=0)
    v_all = jnp.concatenate([vp_ref[...], vc_ref[...]], axis=1)
    for j in range(SWA_BLOCKS_PER_STEP):
        rows = slice(j * BLOCK, (j + 1) * BLOCK)
        kband = k_all[j * BLOCK:(j + 2) * BLOCK]
        vband = v_all[:, j * BLOCK:(j + 2) * BLOCK]
        if j == 0:
            tab = tab_ref[(pl.program_id(1) == 0).astype(jnp.int32)]
        else:
            tab = tab_ref[0]
        q_t = q_ref[rows, :].T
        cols = []
        for h in range(SWA_Q_HEADS):
            qh = q_t[h * hd:(h + 1) * hd]
            cols.append(jnp.concatenate([qh, zeros] if h < SWA_GROUP else [zeros, qh], axis=0))
        s = _dot(kband, jnp.concatenate(cols, axis=1)) + tab
        m = jnp.maximum(jnp.max(s, axis=0, keepdims=True), sink)
        p = jnp.exp2(s - m).astype(bf16)
        sink_term = jnp.exp2(sink - m)
        outs = []
        for c in range(SWA_KV_HEADS):
            r = _dot(vband[c * LANES:(c + 1) * LANES, :], p[:, c * width:(c + 1) * width])
            outs.append(r[0:hd] / (r[hd:hd + 1] + sink_term[:, c * width:(c + 1) * width]))
        for pair in range(SWA_Q_HEADS // 2):
            halves = []
            for h in (2 * pair, 2 * pair + 1):
                g = h % SWA_GROUP
                halves.append(outs[h // SWA_GROUP][:, g * BLOCK:(g + 1) * BLOCK])
            o_ref[rows, pair * LANES:(pair + 1) * LANES] = (
                jnp.concatenate(halves, axis=0).T.astype(o_ref.dtype))


def _attn_kernel(q_ref, k_ref, v_ref, wu_ref, rel_ref, sink_ref, qs_ref, kp_ref, kc_ref, vp_ref,
                 vc_ref, wo_ref, o_ref, wu_o_ref, os_ref, wo_o_ref, s_ref, vt_ref, tab_ref):
    @pl.when((pl.program_id(0) == 0) & (pl.program_id(1) == 0))
    def _():
        _swa_bias_table(rel_ref, tab_ref)

    _mla_kernel(q_ref, k_ref, v_ref, wu_ref, o_ref, wu_o_ref, s_ref, vt_ref)
    _swa_kernel(rel_ref, sink_ref, qs_ref, kp_ref, kc_ref, vp_ref, vc_ref, wo_ref,
                os_ref, wo_o_ref, tab_ref)


def _out_kernel(x_ref, ya_ref, yb_ref, wa_ref, wb_ref, g_ref, h_ref, m_ref):
    h = x_ref[...] + _dot(ya_ref[...], wa_ref[...]) + _dot(yb_ref[...], wb_ref[...])
    h_ref[...] = h
    m_ref[...] = _rms(h, g_ref[...]).astype(m_ref.dtype)


def _mlp_kernel(m_ref, h_hbm, wu_ref, wd_ref, g_ref, o_ref, hbuf, sem):
    i = pl.program_id(0)
    j = pl.program_id(1)
    last_j = pl.num_programs(1) - 1
    tm = o_ref.shape[0]
    h_copy = pltpu.make_async_copy(h_hbm.at[pl.ds(i * tm, tm), :], hbuf, sem)

    def step(first, last):
        n_chunks = MLP_LAST_CHUNKS if last else 1
        rows = tm // n_chunks
        for c in range(n_chunks):
            r = pl.ds(c * rows, rows)
            u = jnp.maximum(_dot(m_ref[r, :], wu_ref[...]), 0.0)
            acc = _dot((u * u).astype(bf16), wd_ref[...])
            if not first:
                acc = o_ref[r, :] + acc
            if last:
                acc = _rms(acc + hbuf[r, :], g_ref[...])
            o_ref[r, :] = acc

    @pl.when(j == 0)
    def _():
        h_copy.start()
        step(first=True, last=False)

    @pl.when((j > 0) & (j < last_j))
    def _():
        step(first=False, last=False)

    @pl.when(j == last_j)
    def _():
        h_copy.wait()
        step(first=False, last=True)


def _params(sem):
    return pltpu.CompilerParams(dimension_semantics=sem, vmem_limit_bytes=VMEM_LIMIT)


def kernel(x, positions, rel_bias, attn_norm, w_in, q_a_norm, w_q_b, kv_a_norm, w_kv_b, sinks,
           w_out, mlp_norm, w_up, w_down, final_norm):
    batch, seq, d = x.shape
    tokens = batch * seq
    nb = seq // BLOCK
    depth = w_in.shape[0]
    assert depth == 1 and d == D_MODEL and seq % MLA_TQ == 0

    assert w_in.shape[2] == _SRC_END
    n_prep = d // WIN_ROWS
    lat_rows = Q_LORA // n_prep
    assert KV_LORA == Q_LORA
    slab3 = lambda rows, cols: pl.BlockSpec((None, rows, cols), lambda i: (0, i, 0))
    slab2 = lambda rows, cols: pl.BlockSpec((rows, cols), lambda i: (i, 0))
    kv_cols = MLA_HEADS * (QK_NOPE + V_HEAD)
    win, wq, wkv = pl.pallas_call(
        _win_kernel,
        grid=(n_prep,),
        in_specs=[pl.BlockSpec((_SRC_END, WIN_ROWS), lambda i: (0, i)),
                  slab3(lat_rows, MLA_HEADS * (QK_NOPE + QK_ROPE)), slab3(lat_rows, kv_cols)],
        out_specs=[pl.BlockSpec((IN_PAD, WIN_ROWS), lambda i: (0, i)),
                   slab2(lat_rows, MLA_HEADS * QK_PAD), slab2(lat_rows, kv_cols)],
        out_shape=[jax.ShapeDtypeStruct((IN_PAD, d), bf16),
                   jax.ShapeDtypeStruct((Q_LORA, MLA_HEADS * QK_PAD), bf16),
                   jax.ShapeDtypeStruct((KV_LORA, kv_cols), bf16)],
        compiler_params=_params(("arbitrary",)),
        name="winprep",
    )(jnp.transpose(w_in[0]), w_q_b, w_kv_b)

    inv = 1.0 / (ROPE_THETA ** (jnp.arange(0, QK_ROPE, 2, dtype=f32) / QK_ROPE))
    inv128 = jnp.concatenate([inv, inv, jnp.zeros((LANES - QK_ROPE,), f32)])[None, :]

    x2 = x.reshape(tokens, d)
    pos2 = positions.reshape(tokens, 1)
    row = lambda v: v.reshape(1, -1)

    tm = PROJ_TM
    full = lambda shape: pl.BlockSpec(shape, lambda i: (0, 0))
    tok = lambda w: pl.BlockSpec((tm, w), lambda i: (i, 0))
    vt_rows = SWA_KV_HEADS * LANES
    wd_slab = pl.BlockSpec((MLP_HIDDEN // (tokens // tm), d), lambda i: (i, 0))
    q, k, v, qs, ks, vst, wd = pl.pallas_call(
        _proj_kernel,
        grid=(tokens // tm,),
        in_specs=[tok(d), tok(1), full((1, LANES)), full((1, d)), full((IN_PAD, d)),
                  full((1, Q_LORA)), full((Q_LORA, MLA_HEADS * QK_PAD)),
                  full((1, KV_LORA)), full((KV_LORA, MLA_HEADS * (QK_NOPE + V_HEAD))),
                  wd_slab],
        out_specs=[tok(MLA_HEADS * QK_PAD), tok(MLA_HEADS * QK_PAD), tok(MLA_WIDTH),
                   tok(SWA_WIDTH), tok(SWA_KV_WIDTH),
                   pl.BlockSpec((vt_rows, tm), lambda i: (0, i)), wd_slab],
        out_shape=[jax.ShapeDtypeStruct((tokens, MLA_HEADS * QK_PAD), bf16),
                   jax.ShapeDtypeStruct((tokens, MLA_HEADS * QK_PAD), bf16),
                   jax.ShapeDtypeStruct((tokens, MLA_WIDTH), bf16),
                   jax.ShapeDtypeStruct((tokens, SWA_WIDTH), bf16),
                   jax.ShapeDtypeStruct((tokens, SWA_KV_WIDTH), bf16),
                   jax.ShapeDtypeStruct((vt_rows, tokens), bf16),
                   jax.ShapeDtypeStruct((MLP_HIDDEN, d), bf16)],
        compiler_params=_params(("arbitrary",)),
        name="proj",
    )(x2, pos2, inv128, row(attn_norm[0]), win, row(q_a_norm[0]), wq, row(kv_a_norm[0]), wkv,
      w_down[0])

    hps = MLA_HEADS_PER_STEP
    bps = SWA_BLOCKS_PER_STEP
    steps = MLA_HEADS // hps
    assert nb // bps == steps
    rows = bps * BLOCK
    smem = pl.BlockSpec(memory_space=pltpu.SMEM)
    wu_slab = pl.BlockSpec((d // (batch * steps), MLP_HIDDEN), lambda b, t: (b * steps + t, 0))
    wo_slab = pl.BlockSpec(((MLA_WIDTH + SWA_WIDTH) // (batch * steps), d),
                           lambda b, t: (b * steps + t, 0))
    prev = lambda b, t: b * nb + jnp.maximum(t * bps - 1, 0)
    y_mla, wu, y_swa, wo = pl.pallas_call(
        _attn_kernel,
        grid=(batch, steps),
        in_specs=[pl.BlockSpec((seq, hps * QK_PAD), lambda b, t: (b, t)),
                  pl.BlockSpec((seq, hps * QK_PAD), lambda b, t: (b, t)),
                  pl.BlockSpec((seq, hps * V_HEAD), lambda b, t: (b, t)),
                  wu_slab,
                  smem, smem,
                  pl.BlockSpec((rows, SWA_WIDTH), lambda b, t: (b * steps + t, 0)),
                  pl.BlockSpec((BLOCK, SWA_KV_WIDTH), lambda b, t: (prev(b, t), 0)),
                  pl.BlockSpec((rows, SWA_KV_WIDTH), lambda b, t: (b * steps + t, 0)),
                  pl.BlockSpec((vt_rows, BLOCK), lambda b, t: (0, prev(b, t))),
                  pl.BlockSpec((vt_rows, rows), lambda b, t: (0, b * steps + t)),
                  wo_slab],
        out_specs=[pl.BlockSpec((seq, hps * V_HEAD), lambda b, t: (b, t)), wu_slab,
                   pl.BlockSpec((rows, SWA_WIDTH), lambda b, t: (b * steps + t, 0)), wo_slab],
        out_shape=[jax.ShapeDtypeStruct((tokens, MLA_WIDTH), bf16),
                   jax.ShapeDtypeStruct((d, MLP_HIDDEN), bf16),
                   jax.ShapeDtypeStruct((tokens, SWA_WIDTH), bf16),
                   jax.ShapeDtypeStruct((MLA_WIDTH + SWA_WIDTH, d), bf16)],
        scratch_shapes=[pltpu.VMEM((hps, 2, seq, MLA_TQ), f32),
                        pltpu.VMEM((hps, VT_ROWS, seq), bf16),
                        pltpu.VMEM((2, 2 * BLOCK, SWA_Q_HEADS * BLOCK), f32)],
        compiler_params=_params(("arbitrary", "arbitrary")),
        name="attn",
    )(q, k, v, w_up[0], rel_bias, row(sinks[0]), qs, ks, ks, vst, vst, w_out[0])

    tm = OUT_TM
    h1, m = pl.pallas_call(
        _out_kernel,
        grid=(tokens // tm,),
        in_specs=[tok(d), tok(MLA_WIDTH), tok(SWA_WIDTH),
                  pl.BlockSpec((MLA_WIDTH, d), lambda i: (0, 0)),
                  pl.BlockSpec((SWA_WIDTH, d), lambda i: (1, 0)), full((1, d))],
        out_specs=[tok(d), tok(d)],
        out_shape=[jax.ShapeDtypeStruct((tokens, d), f32),
                   jax.ShapeDtypeStruct((tokens, d), bf16)],
        compiler_params=_params(("arbitrary",)),
        name="outproj",
    )(x2, y_mla, y_swa, wo, wo, row(mlp_norm[0]))

    tm, th = MLP_TM, MLP_TH
    assert MLP_HIDDEN // th >= 2
    out = pl.pallas_call(
        _mlp_kernel,
        grid=(tokens // tm, MLP_HIDDEN // th),
        in_specs=[pl.BlockSpec((tm, d), lambda i, j: (i, 0)),
                  pl.BlockSpec(memory_space=pl.ANY),
                  pl.BlockSpec((d, th), lambda i, j: (0, j)),
                  pl.BlockSpec((th, d), lambda i, j: (j, 0)),
                  pl.BlockSpec((1, d), lambda i, j: (0, 0))],
        out_specs=pl.BlockSpec((tm, d), lambda i, j: (i, 0)),
        out_shape=jax.ShapeDtypeStruct((tokens, d), f32),
        scratch_shapes=[pltpu.VMEM((tm, d), f32), pltpu.SemaphoreType.DMA(())],
        compiler_params=pltpu.CompilerParams(
            dimension_semantics=("arbitrary", "arbitrary"), vmem_limit_bytes=MLP_VMEM_LIMIT),
        name="mlp",
    )(m, h1, wu, wd, row(final_norm))

    return out.reshape(batch, seq, d)
```

```python
import jax
import jax.numpy as jnp
import numpy as np
from jax import lax
from jax.experimental import pallas as pl
from jax.experimental.pallas import tpu as pltpu

D_MODEL = 2048
MLA_HEADS = 8
QK_NOPE = 128
QK_ROPE = 64
V_HEAD = 128
Q_LORA = 512
KV_LORA = 512
ROPE_THETA = 10000.0
SWA_Q_HEADS = 16
SWA_KV_HEADS = 2
SWA_GROUP = SWA_Q_HEADS // SWA_KV_HEADS
SWA_HEAD_DIM = 64
WINDOW = 128
BLOCK = 128
REL_BUCKETS = 32
REL_MAX_DIST = 128
MLP_HIDDEN = 4 * D_MODEL
EPS = 1e-6
MLA_WIDTH = MLA_HEADS * V_HEAD
SWA_WIDTH = SWA_Q_HEADS * SWA_HEAD_DIM
SWA_KV_WIDTH = SWA_KV_HEADS * SWA_HEAD_DIM

LANES = 128
QK_PAD = 2 * LANES
BF16_SUBLANES = 16
VT_ROWS = V_HEAD + BF16_SUBLANES
LOG2E = float(np.log2(np.e))
MLA_SCALE = (QK_NOPE + QK_ROPE) ** -0.5 * LOG2E
SWA_SCALE = SWA_HEAD_DIM ** -0.5 * LOG2E

_C_Q = 0
_C_KV = _C_Q + Q_LORA
_C_QS = _C_KV + KV_LORA
_C_KS = _C_QS + SWA_WIDTH
_C_VS = _C_KS + SWA_KV_WIDTH
_C_KR = _C_VS + SWA_KV_WIDTH
IN_PAD = _C_KR + LANES

def _t5_bucket_starts():
    max_exact = REL_BUCKETS // 2
    dist = np.arange(WINDOW)
    val = (np.log(np.maximum(dist, 1).astype(np.float32) / np.float32(max_exact))
           / np.float32(np.log(REL_MAX_DIST / max_exact)) * np.float32(REL_BUCKETS - max_exact))
    margin = np.abs(val - np.round(val))[max_exact + 1:]
    assert margin.min() > 1e-3, margin.min()
    large = np.minimum(max_exact + np.floor(val).astype(np.int64), REL_BUCKETS - 1)
    bucket = np.where(dist < max_exact, dist, large)
    assert np.all(np.diff(bucket) >= 0)
    return tuple(int(np.argmax(bucket >= b)) if np.any(bucket >= b) else WINDOW
                 for b in range(REL_BUCKETS))


T5_BUCKET_START = _t5_bucket_starts()

VMEM_LIMIT = 56 * 1024 * 1024

WIN_ROWS = 256
PROJ_TM = 512
MLA_TQ = 256
MLA_HEADS_PER_STEP = 2
SWA_BLOCKS_PER_STEP = 4
OUT_TM = 512
MLP_TM = 1024
MLP_TH = 1024
MLP_LAST_CHUNKS = 2
MLP_VMEM_LIMIT = 60 * 1024 * 1024

bf16 = jnp.bfloat16
f32 = jnp.float32


def _rms(x, g):
    return x * lax.rsqrt(jnp.mean(x * x, axis=-1, keepdims=True) + EPS) * g


def _dot(a, b):
    return jnp.dot(a, b, preferred_element_type=f32)


def _dot_nt(a, b):
    return lax.dot_general(a, b, (((1,), (1,)), ((), ())), preferred_element_type=f32)


def _rope128(t, cos, sin):
    lane = lax.broadcasted_iota(jnp.int32, t.shape, 1)
    fwd = pltpu.roll(t, QK_ROPE // 2, axis=1)
    bwd = pltpu.roll(t, LANES - QK_ROPE // 2, axis=1)
    rot = jnp.where((lane & (QK_ROPE // 2)) == 0, -bwd, fwd)
    return t * cos + rot * sin


_SRC_KR = Q_LORA + KV_LORA
_SRC_QS = _SRC_KR + QK_ROPE
_SRC_KS = _SRC_QS + SWA_WIDTH
_SRC_END = _SRC_KS + 2 * SWA_KV_WIDTH


def _win_kernel(w_ref, wq_ref, wkv_ref, o_ref, wq_o_ref, wkv_o_ref):
    wkv_o_ref[...] = wkv_ref[...].astype(bf16)
    hd = QK_NOPE + QK_ROPE
    for h in range(MLA_HEADS):
        wq_o_ref[:, h * QK_PAD:h * QK_PAD + hd] = wq_ref[:, h * hd:(h + 1) * hd].astype(bf16)
        wq_o_ref[:, h * QK_PAD + hd:(h + 1) * QK_PAD] = jnp.zeros(
            (wq_o_ref.shape[0], QK_PAD - hd), bf16)
    o_ref[_C_Q:_C_QS, :] = w_ref[0:_SRC_KR, :].astype(bf16)
    o_ref[_C_QS:_C_KS, :] = w_ref[_SRC_QS:_SRC_KS, :].astype(bf16)
    o_ref[_C_KS:_C_KR, :] = w_ref[_SRC_KS:_SRC_END, :].astype(bf16)
    o_ref[_C_KR:_C_KR + QK_ROPE, :] = w_ref[_SRC_KR:_SRC_QS, :].astype(bf16)
    o_ref[_C_KR + QK_ROPE:IN_PAD, :] = jnp.zeros(
        (IN_PAD - _C_KR - QK_ROPE, o_ref.shape[1]), bf16)


def _proj_kernel(x_ref, pos_ref, inv_ref, gin_ref, win_ref, gq_ref, wq_ref, gkv_ref, wkv_ref,
                 wd_ref, q_ref, k_ref, v_ref, qs_ref, ks_ref, vst_ref, wd_o_ref):
    wd_o_ref[...] = wd_ref[...].astype(bf16)
    a = _rms(x_ref[...], gin_ref[...])
    proj = _dot_nt(a.astype(bf16), win_ref[...])

    ang = pos_ref[...].astype(f32) * inv_ref[...]
    cos = jnp.cos(ang)
    sin = jnp.sin(ang)

    tm = proj.shape[0]
    qs_ref[...] = (proj[:, _C_QS:_C_QS + SWA_WIDTH] * SWA_SCALE).astype(bf16)
    ks_ref[...] = proj[:, _C_KS:_C_KS + LANES].astype(bf16)
    vt = proj[:, _C_VS:_C_VS + LANES].T.astype(bf16)
    ones = jnp.ones((SWA_HEAD_DIM, tm), bf16)
    for kvh in range(SWA_KV_HEADS):
        v_t = vt[kvh * SWA_HEAD_DIM:(kvh + 1) * SWA_HEAD_DIM]
        vst_ref[kvh * LANES:(kvh + 1) * LANES, :] = jnp.concatenate([v_t, ones], axis=0)

    cq = _rms(proj[:, _C_Q:_C_Q + Q_LORA], gq_ref[...])
    qf = _dot(cq.astype(bf16), wq_ref[...])
    ckv = _rms(proj[:, _C_KV:_C_KV + KV_LORA], gkv_ref[...])
    kvf = _dot(ckv.astype(bf16), wkv_ref[...])
    kr = _rope128(proj[:, _C_KR:_C_KR + LANES], cos, sin).astype(bf16)
    for h in range(MLA_HEADS):
        c0 = h * QK_PAD
        q_ref[:, c0:c0 + LANES] = (qf[:, c0:c0 + LANES] * MLA_SCALE).astype(bf16)
        q_ref[:, c0 + LANES:c0 + QK_PAD] = (
            _rope128(qf[:, c0 + LANES:c0 + QK_PAD], cos, sin) * MLA_SCALE).astype(bf16)
        k_ref[:, c0:c0 + LANES] = kvf[:, c0:c0 + LANES].astype(bf16)
        k_ref[:, c0 + LANES:c0 + QK_PAD] = kr
        v_ref[:, h * V_HEAD:(h + 1) * V_HEAD] = kvf[:, c0 + LANES:c0 + QK_PAD].astype(bf16)


def _mla_steps(q_ref, k_ref, v_ref, wu_ref, o_ref, wu_o_ref, s_ref, vt_ref):
    seq = q_ref.shape[0]
    tq = MLA_TQ
    n_tiles = seq // tq
    heads = range(MLA_HEADS_PER_STEP)
    order = list(reversed(range(n_tiles)))

    def scores(g, i):
        l = (i + 1) * tq
        s_ref[g, i % 2, 0:l, :] = _dot_nt(k_ref[0:l, g * QK_PAD:(g + 1) * QK_PAD],
                                          q_ref[i * tq:(i + 1) * tq, g * QK_PAD:(g + 1) * QK_PAD])

    for g in heads:
        vt_ref[g, 0:V_HEAD, :] = v_ref[:, g * V_HEAD:(g + 1) * V_HEAD].T
        vt_ref[g, V_HEAD:, :] = jnp.ones((VT_ROWS - V_HEAD, seq), bf16)
        scores(g, order[0])
    piece = wu_ref.shape[0] // n_tiles
    for step, i in enumerate(order):
        rows = pl.ds(step * piece, piece)
        wu_o_ref[rows, :] = wu_ref[rows, :].astype(bf16)
        for g in heads:
            if step + 1 < n_tiles:
                scores(g, order[step + 1])
            l0 = i * tq
            key = lax.broadcasted_iota(jnp.int32, (tq, tq), 0)
            qry = lax.broadcasted_iota(jnp.int32, (tq, tq), 1)
            sd = jnp.where(key <= qry, s_ref[g, i % 2, l0:l0 + tq, :], -jnp.inf)
            m = jnp.max(sd, axis=0, keepdims=True)
            if i > 0:
                sa = s_ref[g, i % 2, 0:l0, :]
                m = jnp.maximum(m, jnp.max(sa, axis=0, keepdims=True))
            acc = _dot(vt_ref[g, :, l0:l0 + tq], jnp.exp2(sd - m).astype(bf16))
            if i > 0:
                acc = acc + _dot(vt_ref[g, :, 0:l0], jnp.exp2(sa - m).astype(bf16))
            o_ref[l0:l0 + tq, g * V_HEAD:(g + 1) * V_HEAD] = (
                acc[0:V_HEAD] / acc[V_HEAD:V_HEAD + 1]).T.astype(o_ref.dtype)
        yield


def _swa_bias_table(rel_ref, tab_ref):
    kj = lax.broadcasted_iota(jnp.int32, (2 * BLOCK, BLOCK), 0)
    qi = lax.broadcasted_iota(jnp.int32, (2 * BLOCK, BLOCK), 1) + BLOCK
    dist = qi - kj
    in_window = (dist >= 0) & (dist < WINDOW)
    for h in range(SWA_Q_HEADS):
        acc = jnp.full((2 * BLOCK, BLOCK), rel_ref[0, h] * LOG2E, f32)
        for b in range(1, REL_BUCKETS):
            if T5_BUCKET_START[b] < WINDOW:
                acc = jnp.where(dist >= T5_BUCKET_START[b], rel_ref[b, h] * LOG2E, acc)
        tab_ref[0, :, h * BLOCK:(h + 1) * BLOCK] = jnp.where(in_window, acc, -jnp.inf)
        tab_ref[1, :, h * BLOCK:(h + 1) * BLOCK] = jnp.where(
            in_window & (kj >= BLOCK), acc, -jnp.inf)


def _swa_steps(sink_ref, q_ref, kp_ref, kc_ref, vp_ref, vc_ref, wo_ref, o_ref, wo_o_ref, tab_ref):
    wo_o_ref[...] = wo_ref[...].astype(bf16)
    hd = SWA_HEAD_DIM
    width = SWA_GROUP * BLOCK
    zeros = jnp.zeros((hd, BLOCK), bf16)
    sink = jnp.concatenate(
        [jnp.full((1, BLOCK), sink_ref[0, h] * LOG2E, f32) for h in range(SWA_Q_HEADS)], axis=1)
    k_all = jnp.concatenate([kp_ref[...], kc_ref[...]], axis=0)
    v_all = jnp.concatenate([vp_ref[...], vc_ref[...]], axis=1)
    for j in range(SWA_BLOCKS_PER_STEP):
        rows = slice(j * BLOCK, (j + 1) * BLOCK)
        kband = k_all[j * BLOCK:(j + 2) * BLOCK]
        vband = v_all[:, j * BLOCK:(j + 2) * BLOCK]
        if j == 0:
            tab = tab_ref[(pl.program_id(1) == 0).astype(jnp.int32)]
        else:
            tab = tab_ref[0]
        q_t = q_ref[rows, :].T
        cols = []
        for h in range(SWA_Q_HEADS):
            qh = q_t[h * hd:(h + 1) * hd]
            cols.append(jnp.concatenate([qh, zeros] if h < SWA_GROUP else [zeros, qh], axis=0))
        s = _dot(kband, jnp.concatenate(cols, axis=1)) + tab
        m = jnp.maximum(jnp.max(s, axis=0, keepdims=True), sink)
        p = jnp.exp2(s - m).astype(bf16)
        sink_term = jnp.exp2(sink - m)
        outs = []
        for c in range(SWA_KV_HEADS):
            r = _dot(vband[c * LANES:(c + 1) * LANES, :], p[:, c * width:(c + 1) * width])
            outs.append(r[0:hd] / (r[hd:hd + 1] + sink_term[:, c * width:(c + 1) * width]))
        for pair in range(SWA_Q_HEADS // 2):
            halves = []
            for h in (2 * pair, 2 * pair + 1):
                g = h % SWA_GROUP
                halves.append(outs[h // SWA_GROUP][:, g * BLOCK:(g + 1) * BLOCK])
            o_ref[rows, pair * LANES:(pair + 1) * LANES] = (
                jnp.concatenate(halves, axis=0).T.astype(o_ref.dtype))
        yield


def _attn_kernel(q_ref, k_ref, v_ref, wu_ref, rel_ref, sink_ref, qs_ref, kp_ref, kc_ref, vp_ref,
                 vc_ref, wo_ref, o_ref, wu_o_ref, os_ref, wo_o_ref, s_ref, vt_ref, tab_ref):
    @pl.when((pl.program_id(0) == 0) & (pl.program_id(1) == 0))
    def _():
        _swa_bias_table(rel_ref, tab_ref)

    mla = _mla_steps(q_ref, k_ref, v_ref, wu_ref, o_ref, wu_o_ref, s_ref, vt_ref)
    swa = _swa_steps(sink_ref, qs_ref, kp_ref, kc_ref, vp_ref, vc_ref, wo_ref,
                     os_ref, wo_o_ref, tab_ref)
    n_mla = q_ref.shape[0] // MLA_TQ
    every = n_mla // SWA_BLOCKS_PER_STEP
    for t in range(n_mla):
        next(mla)
        if t % every == every - 1:
            next(swa)


def _out_kernel(x_ref, ya_ref, yb_ref, wa_ref, wb_ref, g_ref, h_ref, m_ref):
    h = x_ref[...] + _dot(ya_ref[...], wa_ref[...]) + _dot(yb_ref[...], wb_ref[...])
    h_ref[...] = h
    m_ref[...] = _rms(h, g_ref[...]).astype(m_ref.dtype)


def _mlp_kernel(m_ref, h_hbm, wu_ref, wd_ref, g_ref, o_ref, hbuf, sem):
    i = pl.program_id(0)
    j = pl.program_id(1)
    last_j = pl.num_programs(1) - 1
    tm = o_ref.shape[0]
    h_copy = pltpu.make_async_copy(h_hbm.at[pl.ds(i * tm, tm), :], hbuf, sem)

    def step(first, last):
        n_chunks = MLP_LAST_CHUNKS if last else 1
        rows = tm // n_chunks
        for c in range(n_chunks):
            r = pl.ds(c * rows, rows)
            u = jnp.maximum(_dot(m_ref[r, :], wu_ref[...]), 0.0)
            acc = _dot((u * u).astype(bf16), wd_ref[...])
            if not first:
                acc = o_ref[r, :] + acc
            if last:
                acc = _rms(acc + hbuf[r, :], g_ref[...])
            o_ref[r, :] = acc

    @pl.when(j == 0)
    def _():
        h_copy.start()
        step(first=True, last=False)

    @pl.when((j > 0) & (j < last_j))
    def _():
        step(first=False, last=False)

    @pl.when(j == last_j)
    def _():
        h_copy.wait()
        step(first=False, last=True)


def _params(sem):
    return pltpu.CompilerParams(dimension_semantics=sem, vmem_limit_bytes=VMEM_LIMIT)


def kernel(x, positions, rel_bias, attn_norm, w_in, q_a_norm, w_q_b, kv_a_norm, w_kv_b, sinks,
           w_out, mlp_norm, w_up, w_down, final_norm):
    batch, seq, d = x.shape
    tokens = batch * seq
    nb = seq // BLOCK
    depth = w_in.shape[0]
    assert depth == 1 and d == D_MODEL and seq % MLA_TQ == 0

    assert w_in.shape[2] == _SRC_END
    n_prep = d // WIN_ROWS
    lat_rows = Q_LORA // n_prep
    assert KV_LORA == Q_LORA
    slab3 = lambda rows, cols: pl.BlockSpec((None, rows, cols), lambda i: (0, i, 0))
    slab2 = lambda rows, cols: pl.BlockSpec((rows, cols), lambda i: (i, 0))
    kv_cols = MLA_HEADS * (QK_NOPE + V_HEAD)
    win, wq, wkv = pl.pallas_call(
        _win_kernel,
        grid=(n_prep,),
        in_specs=[pl.BlockSpec((_SRC_END, WIN_ROWS), lambda i: (0, i)),
                  slab3(lat_rows, MLA_HEADS * (QK_NOPE + QK_ROPE)), slab3(lat_rows, kv_cols)],
        out_specs=[pl.BlockSpec((IN_PAD, WIN_ROWS), lambda i: (0, i)),
                   slab2(lat_rows, MLA_HEADS * QK_PAD), slab2(lat_rows, kv_cols)],
        out_shape=[jax.ShapeDtypeStruct((IN_PAD, d), bf16),
                   jax.ShapeDtypeStruct((Q_LORA, MLA_HEADS * QK_PAD), bf16),
                   jax.ShapeDtypeStruct((KV_LORA, kv_cols), bf16)],
        compiler_params=_params(("arbitrary",)),
        name="winprep",
    )(jnp.transpose(w_in[0]), w_q_b, w_kv_b)

    inv = 1.0 / (ROPE_THETA ** (jnp.arange(0, QK_ROPE, 2, dtype=f32) / QK_ROPE))
    inv128 = jnp.concatenate([inv, inv, jnp.zeros((LANES - QK_ROPE,), f32)])[None, :]

    x2 = x.reshape(tokens, d)
    pos2 = positions.reshape(tokens, 1)
    row = lambda v: v.reshape(1, -1)

    tm = PROJ_TM
    full = lambda shape: pl.BlockSpec(shape, lambda i: (0, 0))
    tok = lambda w: pl.BlockSpec((tm, w), lambda i: (i, 0))
    vt_rows = SWA_KV_HEADS * LANES
    wd_slab = pl.BlockSpec((MLP_HIDDEN // (tokens // tm), d), lambda i: (i, 0))
    q, k, v, qs, ks, vst, wd = pl.pallas_call(
        _proj_kernel,
        grid=(tokens // tm,),
        in_specs=[tok(d), tok(1), full((1, LANES)), full((1, d)), full((IN_PAD, d)),
                  full((1, Q_LORA)), full((Q_LORA, MLA_HEADS * QK_PAD)),
                  full((1, KV_LORA)), full((KV_LORA, MLA_HEADS * (QK_NOPE + V_HEAD))),
                  wd_slab],
        out_specs=[tok(MLA_HEADS * QK_PAD), tok(MLA_HEADS * QK_PAD), tok(MLA_WIDTH),
                   tok(SWA_WIDTH), tok(SWA_KV_WIDTH),
                   pl.BlockSpec((vt_rows, tm), lambda i: (0, i)), wd_slab],
        out_shape=[jax.ShapeDtypeStruct((tokens, MLA_HEADS * QK_PAD), bf16),
                   jax.ShapeDtypeStruct((tokens, MLA_HEADS * QK_PAD), bf16),
                   jax.ShapeDtypeStruct((tokens, MLA_WIDTH), bf16),
                   jax.ShapeDtypeStruct((tokens, SWA_WIDTH), bf16),
                   jax.ShapeDtypeStruct((tokens, SWA_KV_WIDTH), bf16),
                   jax.ShapeDtypeStruct((vt_rows, tokens), bf16),
                   jax.ShapeDtypeStruct((MLP_HIDDEN, d), bf16)],
        compiler_params=_params(("arbitrary",)),
        name="proj",
    )(x2, pos2, inv128, row(attn_norm[0]), win, row(q_a_norm[0]), wq, row(kv_a_norm[0]), wkv,
      w_down[0])

    hps = MLA_HEADS_PER_STEP
    bps = SWA_BLOCKS_PER_STEP
    steps = MLA_HEADS // hps
    assert nb // bps == steps
    rows = bps * BLOCK
    smem = pl.BlockSpec(memory_space=pltpu.SMEM)
    wu_slab = pl.BlockSpec((d // (batch * steps), MLP_HIDDEN), lambda b, t: (b * steps + t, 0))
    wo_slab = pl.BlockSpec(((MLA_WIDTH + SWA_WIDTH) // (batch * steps), d),
                           lambda b, t: (b * steps + t, 0))
    prev = lambda b, t: b * nb + jnp.maximum(t * bps - 1, 0)
    y_mla, wu, y_swa, wo = pl.pallas_call(
        _attn_kernel,
        grid=(batch, steps),
        in_specs=[pl.BlockSpec((seq, hps * QK_PAD), lambda b, t: (b, t)),
                  pl.BlockSpec((seq, hps * QK_PAD), lambda b, t: (b, t)),
                  pl.BlockSpec((seq, hps * V_HEAD), lambda b, t: (b, t)),
                  wu_slab,
                  smem, smem,
                  pl.BlockSpec((rows, SWA_WIDTH), lambda b, t: (b * steps + t, 0)),
                  pl.BlockSpec((BLOCK, SWA_KV_WIDTH), lambda b, t: (prev(b, t), 0)),
                  pl.BlockSpec((rows, SWA_KV_WIDTH), lambda b, t: (b * steps + t, 0)),
                  pl.BlockSpec((vt_rows, BLOCK), lambda b, t: (0, prev(b, t))),
                  pl.BlockSpec((vt_rows, rows), lambda b, t: (0, b * steps + t)),
                  wo_slab],
        out_specs=[pl.BlockSpec((seq, hps * V_HEAD), lambda b, t: (b, t)), wu_slab,
                   pl.BlockSpec((rows, SWA_WIDTH), lambda b, t: (b * steps + t, 0)), wo_slab],
        out_shape=[jax.ShapeDtypeStruct((tokens, MLA_WIDTH), bf16),
                   jax.ShapeDtypeStruct((d, MLP_HIDDEN), bf16),
                   jax.ShapeDtypeStruct((tokens, SWA_WIDTH), bf16),
                   jax.ShapeDtypeStruct((MLA_WIDTH + SWA_WIDTH, d), bf16)],
        scratch_shapes=[pltpu.VMEM((hps, 2, seq, MLA_TQ), f32),
                        pltpu.VMEM((hps, VT_ROWS, seq), bf16),
                        pltpu.VMEM((2, 2 * BLOCK, SWA_Q_HEADS * BLOCK), f32)],
        compiler_params=_params(("arbitrary", "arbitrary")),
        name="attn",
    )(q, k, v, w_up[0], rel_bias, row(sinks[0]), qs, ks, ks, vst, vst, w_out[0])

    tm = OUT_TM
    h1, m = pl.pallas_call(
        _out_kernel,
        grid=(tokens // tm,),
        in_specs=[tok(d), tok(MLA_WIDTH), tok(SWA_WIDTH),
                  pl.BlockSpec((MLA_WIDTH, d), lambda i: (0, 0)),
                  pl.BlockSpec((SWA_WIDTH, d), lambda i: (1, 0)), full((1, d))],
        out_specs=[tok(d), tok(d)],
        out_shape=[jax.ShapeDtypeStruct((tokens, d), f32),
                   jax.ShapeDtypeStruct((tokens, d), bf16)],
        compiler_params=_params(("arbitrary",)),
        name="outproj",
    )(x2, y_mla, y_swa, wo, wo, row(mlp_norm[0]))

    tm, th = MLP_TM, MLP_TH
    assert MLP_HIDDEN // th >= 2
    out = pl.pallas_call(
        _mlp_kernel,
        grid=(tokens // tm, MLP_HIDDEN // th),
        in_specs=[pl.BlockSpec((tm, d), lambda i, j: (i, 0)),
                  pl.BlockSpec(memory_space=pl.ANY),
                  pl.BlockSpec((d, th), lambda i, j: (0, j)),
                  pl.BlockSpec((th, d), lambda i, j: (j, 0)),
                  pl.BlockSpec((1, d), lambda i, j: (0, 0))],
        out_specs=pl.BlockSpec((tm, d), lambda i, j: (i, 0)),
        out_shape=jax.ShapeDtypeStruct((tokens, d), f32),
        scratch_shapes=[pltpu.VMEM((tm, d), f32), pltpu.SemaphoreType.DMA(())],
        compiler_params=pltpu.CompilerParams(
            dimension_semantics=("arbitrary", "arbitrary"), vmem_limit_bytes=MLP_VMEM_LIMIT),
        name="mlp",
    )(m, h1, wu, wd, row(final_norm))

    return out.reshape(batch, seq, d)
```

```python
import jax
import jax.numpy as jnp
import numpy as np
from jax import lax
from jax.experimental import pallas as pl
from jax.experimental.pallas import tpu as pltpu

D_MODEL = 2048
MLA_HEADS = 8
QK_NOPE = 128
QK_ROPE = 64
V_HEAD = 128
Q_LORA = 512
KV_LORA = 512
ROPE_THETA = 10000.0
SWA_Q_HEADS = 16
SWA_KV_HEADS = 2
SWA_GROUP = SWA_Q_HEADS // SWA_KV_HEADS
SWA_HEAD_DIM = 64
WINDOW = 128
BLOCK = 128
REL_BUCKETS = 32
REL_MAX_DIST = 128
MLP_HIDDEN = 4 * D_MODEL
EPS = 1e-6
MLA_WIDTH = MLA_HEADS * V_HEAD
SWA_WIDTH = SWA_Q_HEADS * SWA_HEAD_DIM
SWA_KV_WIDTH = SWA_KV_HEADS * SWA_HEAD_DIM

LANES = 128
QK_PAD = 2 * LANES
BF16_SUBLANES = 16
VT_ROWS = V_HEAD + BF16_SUBLANES
LOG2E = float(np.log2(np.e))
MLA_SCALE = (QK_NOPE + QK_ROPE) ** -0.5 * LOG2E
SWA_SCALE = SWA_HEAD_DIM ** -0.5 * LOG2E

_C_Q = 0
_C_KV = _C_Q + Q_LORA
_C_QS = _C_KV + KV_LORA
_C_KS = _C_QS + SWA_WIDTH
_C_VS = _C_KS + SWA_KV_WIDTH
_C_KR = _C_VS + SWA_KV_WIDTH
IN_PAD = _C_KR + LANES

def _t5_bucket_starts():
    max_exact = REL_BUCKETS // 2
    dist = np.arange(WINDOW)
    val = (np.log(np.maximum(dist, 1).astype(np.float32) / np.float32(max_exact))
           / np.float32(np.log(REL_MAX_DIST / max_exact)) * np.float32(REL_BUCKETS - max_exact))
    margin = np.abs(val - np.round(val))[max_exact + 1:]
    assert margin.min() > 1e-3, margin.min()
    large = np.minimum(max_exact + np.floor(val).astype(np.int64), REL_BUCKETS - 1)
    bucket = np.where(dist < max_exact, dist, large)
    assert np.all(np.diff(bucket) >= 0)
    return tuple(int(np.argmax(bucket >= b)) if np.any(bucket >= b) else WINDOW
                 for b in range(REL_BUCKETS))


T5_BUCKET_START = _t5_bucket_starts()

VMEM_LIMIT = 56 * 1024 * 1024

WIN_ROWS = 256
PROJ_TM = 512
MLA_TQ = 256
MLA_HEADS_PER_STEP = 2
MLA_STAGES = 4
SWA_BLOCKS_PER_STEP = 4
OUT_TM = 512
MLP_TM = 1024
MLP_TH = 1024
MLP_LAST_CHUNKS = 2
MLP_VMEM_LIMIT = 60 * 1024 * 1024

bf16 = jnp.bfloat16
f32 = jnp.float32


def _rms(x, g):
    return x * lax.rsqrt(jnp.mean(x * x, axis=-1, keepdims=True) + EPS) * g


def _dot(a, b):
    return jnp.dot(a, b, preferred_element_type=f32)


def _dot_nt(a, b):
    return lax.dot_general(a, b, (((1,), (1,)), ((), ())), preferred_element_type=f32)


def _rope128(t, cos, sin):
    lane = lax.broadcasted_iota(jnp.int32, t.shape, 1)
    fwd = pltpu.roll(t, QK_ROPE // 2, axis=1)
    bwd = pltpu.roll(t, LANES - QK_ROPE // 2, axis=1)
    rot = jnp.where((lane & (QK_ROPE // 2)) == 0, -bwd, fwd)
    return t * cos + rot * sin


_SRC_KR = Q_LORA + KV_LORA
_SRC_QS = _SRC_KR + QK_ROPE
_SRC_KS = _SRC_QS + SWA_WIDTH
_SRC_END = _SRC_KS + 2 * SWA_KV_WIDTH


def _win_kernel(w_ref, wq_ref, wkv_ref, o_ref, wq_o_ref, wkv_o_ref):
    wkv_o_ref[...] = wkv_ref[...].astype(bf16)
    hd = QK_NOPE + QK_ROPE
    for h in range(MLA_HEADS):
        wq_o_ref[:, h * QK_PAD:h * QK_PAD + hd] = wq_ref[:, h * hd:(h + 1) * hd].astype(bf16)
        wq_o_ref[:, h * QK_PAD + hd:(h + 1) * QK_PAD] = jnp.zeros(
            (wq_o_ref.shape[0], QK_PAD - hd), bf16)
    o_ref[_C_Q:_C_QS, :] = w_ref[0:_SRC_KR, :].astype(bf16)
    o_ref[_C_QS:_C_KS, :] = w_ref[_SRC_QS:_SRC_KS, :].astype(bf16)
    o_ref[_C_KS:_C_KR, :] = w_ref[_SRC_KS:_SRC_END, :].astype(bf16)
    o_ref[_C_KR:_C_KR + QK_ROPE, :] = w_ref[_SRC_KR:_SRC_QS, :].astype(bf16)
    o_ref[_C_KR + QK_ROPE:IN_PAD, :] = jnp.zeros(
        (IN_PAD - _C_KR - QK_ROPE, o_ref.shape[1]), bf16)


def _proj_kernel(x_ref, pos_ref, inv_ref, gin_ref, win_ref, gq_ref, wq_ref, gkv_ref, wkv_ref,
                 wd_ref, q_ref, k_ref, v_ref, qs_ref, ks_ref, vst_ref, wd_o_ref):
    wd_o_ref[...] = wd_ref[...].astype(bf16)
    a = _rms(x_ref[...], gin_ref[...])
    proj = _dot_nt(a.astype(bf16), win_ref[...])

    ang = pos_ref[...].astype(f32) * inv_ref[...]
    cos = jnp.cos(ang)
    sin = jnp.sin(ang)

    tm = proj.shape[0]
    qs_ref[...] = (proj[:, _C_QS:_C_QS + SWA_WIDTH] * SWA_SCALE).astype(bf16)
    ks_ref[...] = proj[:, _C_KS:_C_KS + LANES].astype(bf16)
    vt = proj[:, _C_VS:_C_VS + LANES].T.astype(bf16)
    ones = jnp.ones((SWA_HEAD_DIM, tm), bf16)
    for kvh in range(SWA_KV_HEADS):
        v_t = vt[kvh * SWA_HEAD_DIM:(kvh + 1) * SWA_HEAD_DIM]
        vst_ref[kvh * LANES:(kvh + 1) * LANES, :] = jnp.concatenate([v_t, ones], axis=0)

    cq = _rms(proj[:, _C_Q:_C_Q + Q_LORA], gq_ref[...])
    qf = _dot(cq.astype(bf16), wq_ref[...])
    ckv = _rms(proj[:, _C_KV:_C_KV + KV_LORA], gkv_ref[...])
    kvf = _dot(ckv.astype(bf16), wkv_ref[...])
    kr = _rope128(proj[:, _C_KR:_C_KR + LANES], cos, sin).astype(bf16)
    for h in range(MLA_HEADS):
        c0 = h * QK_PAD
        q_ref[:, c0:c0 + LANES] = (qf[:, c0:c0 + LANES] * MLA_SCALE).astype(bf16)
        q_ref[:, c0 + LANES:c0 + QK_PAD] = (
            _rope128(qf[:, c0 + LANES:c0 + QK_PAD], cos, sin) * MLA_SCALE).astype(bf16)
        k_ref[:, c0:c0 + LANES] = kvf[:, c0:c0 + LANES].astype(bf16)
        k_ref[:, c0 + LANES:c0 + QK_PAD] = kr
        v_ref[:, h * V_HEAD:(h + 1) * V_HEAD] = kvf[:, c0 + LANES:c0 + QK_PAD].astype(bf16)


def _mla_steps(q_ref, k_ref, v_ref, wu_ref, o_ref, wu_o_ref, s_ref, vt_ref):
    seq = q_ref.shape[0]
    tq = MLA_TQ
    n_tiles = seq // tq
    heads = range(MLA_HEADS_PER_STEP)
    order = list(reversed(range(n_tiles)))

    def base(i):
        return tq * (i * (i + 1) // 2)

    def scores(g, i):
        l = (i + 1) * tq
        s_ref[g, base(i):base(i) + l, :] = _dot_nt(
            k_ref[0:l, g * QK_PAD:(g + 1) * QK_PAD],
            q_ref[i * tq:(i + 1) * tq, g * QK_PAD:(g + 1) * QK_PAD])

    for g in heads:
        vt_ref[g, 0:V_HEAD, :] = v_ref[:, g * V_HEAD:(g + 1) * V_HEAD].T
        vt_ref[g, V_HEAD:, :] = jnp.ones((VT_ROWS - V_HEAD, seq), bf16)
        for ahead in range(MLA_STAGES - 1):
            scores(g, order[ahead])
    piece = wu_ref.shape[0] // n_tiles
    for step, i in enumerate(order):
        rows = pl.ds(step * piece, piece)
        wu_o_ref[rows, :] = wu_ref[rows, :].astype(bf16)
        for g in heads:
            if step + MLA_STAGES - 1 < n_tiles:
                scores(g, order[step + MLA_STAGES - 1])
            l0 = i * tq
            key = lax.broadcasted_iota(jnp.int32, (tq, tq), 0)
            qry = lax.broadcasted_iota(jnp.int32, (tq, tq), 1)
            sd = jnp.where(key <= qry, s_ref[g, base(i) + l0:base(i) + l0 + tq, :], -jnp.inf)
            m = jnp.max(sd, axis=0, keepdims=True)
            if i > 0:
                sa = s_ref[g, base(i):base(i) + l0, :]
                m = jnp.maximum(m, jnp.max(sa, axis=0, keepdims=True))
            acc = _dot(vt_ref[g, :, l0:l0 + tq], jnp.exp2(sd - m).astype(bf16))
            if i > 0:
                acc = acc + _dot(vt_ref[g, :, 0:l0], jnp.exp2(sa - m).astype(bf16))
            o_ref[l0:l0 + tq, g * V_HEAD:(g + 1) * V_HEAD] = (
                acc[0:V_HEAD] / acc[V_HEAD:V_HEAD + 1]).T.astype(o_ref.dtype)
        yield


def _swa_bias_table(rel_ref, tab_ref):
    kj = lax.broadcasted_iota(jnp.int32, (2 * BLOCK, BLOCK), 0)
    qi = lax.broadcasted_iota(jnp.int32, (2 * BLOCK, BLOCK), 1) + BLOCK
    dist = qi - kj
    in_window = (dist >= 0) & (dist < WINDOW)
    for h in range(SWA_Q_HEADS):
        acc = jnp.full((2 * BLOCK, BLOCK), rel_ref[0, h] * LOG2E, f32)
        for b in range(1, REL_BUCKETS):
            if T5_BUCKET_START[b] < WINDOW:
                acc = jnp.where(dist >= T5_BUCKET_START[b], rel_ref[b, h] * LOG2E, acc)
        tab_ref[0, :, h * BLOCK:(h + 1) * BLOCK] = jnp.where(in_window, acc, -jnp.inf)
        tab_ref[1, :, h * BLOCK:(h + 1) * BLOCK] = jnp.where(
            in_window & (kj >= BLOCK), acc, -jnp.inf)


def _swa_steps(sink_ref, q_ref, kp_ref, kc_ref, vp_ref, vc_ref, wo_ref, o_ref, wo_o_ref, tab_ref):
    wo_o_ref[...] = wo_ref[...].astype(bf16)
    hd = SWA_HEAD_DIM
    width = SWA_GROUP * BLOCK
    zeros = jnp.zeros((hd, BLOCK), bf16)
    sink = jnp.concatenate(
        [jnp.full((1, BLOCK), sink_ref[0, h] * LOG2E, f32) for h in range(SWA_Q_HEADS)], axis=1)
    k_all = jnp.concatenate([kp_ref[...], kc_ref[...]], axis=0)
    v_all = jnp.concatenate([vp_ref[...], vc_ref[...]], axis=1)
    for j in range(SWA_BLOCKS_PER_STEP):
        rows = slice(j * BLOCK, (j + 1) * BLOCK)
        kband = k_all[j * BLOCK:(j + 2) * BLOCK]
        vband = v_all[:, j * BLOCK:(j + 2) * BLOCK]
        if j == 0:
            tab = tab_ref[(pl.program_id(1) == 0).astype(jnp.int32)]
        else:
            tab = tab_ref[0]
        q_t = q_ref[rows, :].T
        cols = []
        for h in range(SWA_Q_HEADS):
            qh = q_t[h * hd:(h + 1) * hd]
            cols.append(jnp.concatenate([qh, zeros] if h < SWA_GROUP else [zeros, qh], axis=0))
        s = _dot(kband, jnp.concatenate(cols, axis=1)) + tab
        m = jnp.maximum(jnp.max(s, axis=0, keepdims=True), sink)
        p = jnp.exp2(s - m).astype(bf16)
        sink_term = jnp.exp2(sink - m)
        outs = []
        for c in range(SWA_KV_HEADS):
            r = _dot(vband[c * LANES:(c + 1) * LANES, :], p[:, c * width:(c + 1) * width])
            outs.append(r[0:hd] / (r[hd:hd + 1] + sink_term[:, c * width:(c + 1) * width]))
        for pair in range(SWA_Q_HEADS // 2):
            halves = []
            for h in (2 * pair, 2 * pair + 1):
                g = h % SWA_GROUP
                halves.append(outs[h // SWA_GROUP][:, g * BLOCK:(g + 1) * BLOCK])
            o_ref[rows, pair * LANES:(pair + 1) * LANES] = (
                jnp.concatenate(halves, axis=0).T.astype(o_ref.dtype))
        yield


def _attn_kernel(q_ref, k_ref, v_ref, wu_ref, rel_ref, sink_ref, qs_ref, kp_ref, kc_ref, vp_ref,
                 vc_ref, wo_ref, o_ref, wu_o_ref, os_ref, wo_o_ref, s_ref, vt_ref, tab_ref):
    @pl.when((pl.program_id(0) == 0) & (pl.program_id(1) == 0))
    def _():
        _swa_bias_table(rel_ref, tab_ref)

    mla = _mla_steps(q_ref, k_ref, v_ref, wu_ref, o_ref, wu_o_ref, s_ref, vt_ref)
    swa = _swa_steps(sink_ref, qs_ref, kp_ref, kc_ref, vp_ref, vc_ref, wo_ref,
                     os_ref, wo_o_ref, tab_ref)
    n_mla = q_ref.shape[0] // MLA_TQ
    every = n_mla // SWA_BLOCKS_PER_STEP
    for t in range(n_mla):
        next(mla)
        if t % every == every - 1:
            next(swa)


def _out_kernel(x_ref, ya_ref, yb_ref, wa_ref, wb_ref, g_ref, h_ref, m_ref):
    h = x_ref[...] + _dot(ya_ref[...], wa_ref[...]) + _dot(yb_ref[...], wb_ref[...])
    h_ref[...] = h
    m_ref[...] = _rms(h, g_ref[...]).astype(m_ref.dtype)


def _mlp_kernel(m_ref, h_hbm, wu_ref, wd_ref, g_ref, o_ref, hbuf, sem):
    i = pl.program_id(0)
    j = pl.program_id(1)
    last_j = pl.num_programs(1) - 1
    tm = o_ref.shape[0]
    h_copy = pltpu.make_async_copy(h_hbm.at[pl.ds(i * tm, tm), :], hbuf, sem)

    def step(first, last):
        n_chunks = MLP_LAST_CHUNKS if last else 1
        rows = tm // n_chunks
        for c in range(n_chunks):
            r = pl.ds(c * rows, rows)
            u = jnp.maximum(_dot(m_ref[r, :], wu_ref[...]), 0.0)
            acc = _dot((u * u).astype(bf16), wd_ref[...])
            if not first:
                acc = o_ref[r, :] + acc
            if last:
                acc = _rms(acc + hbuf[r, :], g_ref[...])
            o_ref[r, :] = acc

    @pl.when(j == 0)
    def _():
        h_copy.start()
        step(first=True, last=False)

    @pl.when((j > 0) & (j < last_j))
    def _():
        step(first=False, last=False)

    @pl.when(j == last_j)
    def _():
        h_copy.wait()
        step(first=False, last=True)


def _params(sem):
    return pltpu.CompilerParams(dimension_semantics=sem, vmem_limit_bytes=VMEM_LIMIT)


def kernel(x, positions, rel_bias, attn_norm, w_in, q_a_norm, w_q_b, kv_a_norm, w_kv_b, sinks,
           w_out, mlp_norm, w_up, w_down, final_norm):
    batch, seq, d = x.shape
    tokens = batch * seq
    nb = seq // BLOCK
    depth = w_in.shape[0]
    assert depth == 1 and d == D_MODEL and seq % MLA_TQ == 0

    assert w_in.shape[2] == _SRC_END
    n_prep = d // WIN_ROWS
    lat_rows = Q_LORA // n_prep
    assert KV_LORA == Q_LORA
    slab3 = lambda rows, cols: pl.BlockSpec((None, rows, cols), lambda i: (0, i, 0))
    slab2 = lambda rows, cols: pl.BlockSpec((rows, cols), lambda i: (i, 0))
    kv_cols = MLA_HEADS * (QK_NOPE + V_HEAD)
    win, wq, wkv = pl.pallas_call(
        _win_kernel,
        grid=(n_prep,),
        in_specs=[pl.BlockSpec((_SRC_END, WIN_ROWS), lambda i: (0, i)),
                  slab3(lat_rows, MLA_HEADS * (QK_NOPE + QK_ROPE)), slab3(lat_rows, kv_cols)],
        out_specs=[pl.BlockSpec((IN_PAD, WIN_ROWS), lambda i: (0, i)),
                   slab2(lat_rows, MLA_HEADS * QK_PAD), slab2(lat_rows, kv_cols)],
        out_shape=[jax.ShapeDtypeStruct((IN_PAD, d), bf16),
                   jax.ShapeDtypeStruct((Q_LORA, MLA_HEADS * QK_PAD), bf16),
                   jax.ShapeDtypeStruct((KV_LORA, kv_cols), bf16)],
        compiler_params=_params(("arbitrary",)),
        name="winprep",
    )(jnp.transpose(w_in[0]), w_q_b, w_kv_b)

    inv = 1.0 / (ROPE_THETA ** (jnp.arange(0, QK_ROPE, 2, dtype=f32) / QK_ROPE))
    inv128 = jnp.concatenate([inv, inv, jnp.zeros((LANES - QK_ROPE,), f32)])[None, :]

    x2 = x.reshape(tokens, d)
    pos2 = positions.reshape(tokens, 1)
    row = lambda v: v.reshape(1, -1)

    tm = PROJ_TM
    full = lambda shape: pl.BlockSpec(shape, lambda i: (0, 0))
    tok = lambda w: pl.BlockSpec((tm, w), lambda i: (i, 0))
    vt_rows = SWA_KV_HEADS * LANES
    wd_slab = pl.BlockSpec((MLP_HIDDEN // (tokens // tm), d), lambda i: (i, 0))
    q, k, v, qs, ks, vst, wd = pl.pallas_call(
        _proj_kernel,
        grid=(tokens // tm,),
        in_specs=[tok(d), tok(1), full((1, LANES)), full((1, d)), full((IN_PAD, d)),
                  full((1, Q_LORA)), full((Q_LORA, MLA_HEADS * QK_PAD)),
                  full((1, KV_LORA)), full((KV_LORA, MLA_HEADS * (QK_NOPE + V_HEAD))),
                  wd_slab],
        out_specs=[tok(MLA_HEADS * QK_PAD), tok(MLA_HEADS * QK_PAD), tok(MLA_WIDTH),
                   tok(SWA_WIDTH), tok(SWA_KV_WIDTH),
                   pl.BlockSpec((vt_rows, tm), lambda i: (0, i)), wd_slab],
        out_shape=[jax.ShapeDtypeStruct((tokens, MLA_HEADS * QK_PAD), bf16),
                   jax.ShapeDtypeStruct((tokens, MLA_HEADS * QK_PAD), bf16),
                   jax.ShapeDtypeStruct((tokens, MLA_WIDTH), bf16),
                   jax.ShapeDtypeStruct((tokens, SWA_WIDTH), bf16),
                   jax.ShapeDtypeStruct((tokens, SWA_KV_WIDTH), bf16),
                   jax.ShapeDtypeStruct((vt_rows, tokens), bf16),
                   jax.ShapeDtypeStruct((MLP_HIDDEN, d), bf16)],
        compiler_params=_params(("arbitrary",)),
        name="proj",
    )(x2, pos2, inv128, row(attn_norm[0]), win, row(q_a_norm[0]), wq, row(kv_a_norm[0]), wkv,
      w_down[0])

    hps = MLA_HEADS_PER_STEP
    bps = SWA_BLOCKS_PER_STEP
    steps = MLA_HEADS // hps
    assert nb // bps == steps
    rows = bps * BLOCK
    smem = pl.BlockSpec(memory_space=pltpu.SMEM)
    wu_slab = pl.BlockSpec((d // (batch * steps), MLP_HIDDEN), lambda b, t: (b * steps + t, 0))
    wo_slab = pl.BlockSpec(((MLA_WIDTH + SWA_WIDTH) // (batch * steps), d),
                           lambda b, t: (b * steps + t, 0))
    prev = lambda b, t: b * nb + jnp.maximum(t * bps - 1, 0)
    y_mla, wu, y_swa, wo = pl.pallas_call(
        _attn_kernel,
        grid=(batch, steps),
        in_specs=[pl.BlockSpec((seq, hps * QK_PAD), lambda b, t: (b, t)),
                  pl.BlockSpec((seq, hps * QK_PAD), lambda b, t: (b, t)),
                  pl.BlockSpec((seq, hps * V_HEAD), lambda b, t: (b, t)),
                  wu_slab,
                  smem, smem,
                  pl.BlockSpec((rows, SWA_WIDTH), lambda b, t: (b * steps + t, 0)),
                  pl.BlockSpec((BLOCK, SWA_KV_WIDTH), lambda b, t: (prev(b, t), 0)),
                  pl.BlockSpec((rows, SWA_KV_WIDTH), lambda b, t: (b * steps + t, 0)),
                  pl.BlockSpec((vt_rows, BLOCK), lambda b, t: (0, prev(b, t))),
                  pl.BlockSpec((vt_rows, rows), lambda b, t: (0, b * steps + t)),
                  wo_slab],
        out_specs=[pl.BlockSpec((seq, hps * V_HEAD), lambda b, t: (b, t)), wu_slab,
                   pl.BlockSpec((rows, SWA_WIDTH), lambda b, t: (b * steps + t, 0)), wo_slab],
        out_shape=[jax.ShapeDtypeStruct((tokens, MLA_WIDTH), bf16),
                   jax.ShapeDtypeStruct((d, MLP_HIDDEN), bf16),
                   jax.ShapeDtypeStruct((tokens, SWA_WIDTH), bf16),
                   jax.ShapeDtypeStruct((MLA_WIDTH + SWA_WIDTH, d), bf16)],
        scratch_shapes=[pltpu.VMEM((hps, (seq // MLA_TQ) * (seq // MLA_TQ + 1) // 2 * MLA_TQ,
                                    MLA_TQ), f32),
                        pltpu.VMEM((hps, VT_ROWS, seq), bf16),
                        pltpu.VMEM((2, 2 * BLOCK, SWA_Q_HEADS * BLOCK), f32)],
        compiler_params=_params(("arbitrary", "arbitrary")),
        name="attn",
    )(q, k, v, w_up[0], rel_bias, row(sinks[0]), qs, ks, ks, vst, vst, w_out[0])

    tm = OUT_TM
    h1, m = pl.pallas_call(
        _out_kernel,
        grid=(tokens // tm,),
        in_specs=[tok(d), tok(MLA_WIDTH), tok(SWA_WIDTH),
                  pl.BlockSpec((MLA_WIDTH, d), lambda i: (0, 0)),
                  pl.BlockSpec((SWA_WIDTH, d), lambda i: (1, 0)), full((1, d))],
        out_specs=[tok(d), tok(d)],
        out_shape=[jax.ShapeDtypeStruct((tokens, d), f32),
                   jax.ShapeDtypeStruct((tokens, d), bf16)],
        compiler_params=_params(("arbitrary",)),
        name="outproj",
    )(x2, y_mla, y_swa, wo, wo, row(mlp_norm[0]))

    tm, th = MLP_TM, MLP_TH
    assert MLP_HIDDEN // th >= 2
    out = pl.pallas_call(
        _mlp_kernel,
        grid=(tokens // tm, MLP_HIDDEN // th),
        in_specs=[pl.BlockSpec((tm, d), lambda i, j: (i, 0)),
                  pl.BlockSpec(memory_space=pl.ANY),
                  pl.BlockSpec((d, th), lambda i, j: (0, j)),
                  pl.BlockSpec((th, d), lambda i, j: (j, 0)),
                  pl.BlockSpec((1, d), lambda i, j: (0, 0))],
        out_specs=pl.BlockSpec((tm, d), lambda i, j: (i, 0)),
        out_shape=jax.ShapeDtypeStruct((tokens, d), f32),
        scratch_shapes=[pltpu.VMEM((tm, d), f32), pltpu.SemaphoreType.DMA(())],
        compiler_params=pltpu.CompilerParams(
            dimension_semantics=("arbitrary", "arbitrary"), vmem_limit_bytes=MLP_VMEM_LIMIT),
        name="mlp",
    )(m, h1, wu, wd, row(final_norm))

    return out.reshape(batch, seq, d)
```

```python
import jax
import jax.numpy as jnp
import numpy as np
from jax import lax
from jax.experimental import pallas as pl
from jax.experimental.pallas import tpu as pltpu

D_MODEL = 2048
MLA_HEADS = 8
QK_NOPE = 128
QK_ROPE = 64
V_HEAD = 128
Q_LORA = 512
KV_LORA = 512
ROPE_THETA = 10000.0
SWA_Q_HEADS = 16
SWA_KV_HEADS = 2
SWA_GROUP = SWA_Q_HEADS // SWA_KV_HEADS
SWA_HEAD_DIM = 64
WINDOW = 128
BLOCK = 128
REL_BUCKETS = 32
REL_MAX_DIST = 128
MLP_HIDDEN = 4 * D_MODEL
EPS = 1e-6
MLA_WIDTH = MLA_HEADS * V_HEAD
SWA_WIDTH = SWA_Q_HEADS * SWA_HEAD_DIM
SWA_KV_WIDTH = SWA_KV_HEADS * SWA_HEAD_DIM

LANES = 128
QK_PAD = 2 * LANES
BF16_SUBLANES = 16
VT_ROWS = V_HEAD + BF16_SUBLANES
LOG2E = float(np.log2(np.e))
MLA_SCALE = (QK_NOPE + QK_ROPE) ** -0.5 * LOG2E
SWA_SCALE = SWA_HEAD_DIM ** -0.5 * LOG2E

_C_Q = 0
_C_KV = _C_Q + Q_LORA
_C_QS = _C_KV + KV_LORA
_C_KS = _C_QS + SWA_WIDTH
_C_VS = _C_KS + SWA_KV_WIDTH
_C_KR = _C_VS + SWA_KV_WIDTH
IN_PAD = _C_KR + LANES

def _t5_bucket_starts():
    max_exact = REL_BUCKETS // 2
    dist = np.arange(WINDOW)
    val = (np.log(np.maximum(dist, 1).astype(np.float32) / np.float32(max_exact))
           / np.float32(np.log(REL_MAX_DIST / max_exact)) * np.float32(REL_BUCKETS - max_exact))
    margin = np.abs(val - np.round(val))[max_exact + 1:]
    assert margin.min() > 1e-3, margin.min()
    large = np.minimum(max_exact + np.floor(val).astype(np.int64), REL_BUCKETS - 1)
    bucket = np.where(dist < max_exact, dist, large)
    assert np.all(np.diff(bucket) >= 0)
    return tuple(int(np.argmax(bucket >= b)) if np.any(bucket >= b) else WINDOW
                 for b in range(REL_BUCKETS))


T5_BUCKET_START = _t5_bucket_starts()

VMEM_LIMIT = 56 * 1024 * 1024

WIN_ROWS = 256
PROJ_TM = 512
MLA_TQ = 256
MLA_HEADS_PER_STEP = 2
MLA_STAGES = 4
SWA_BLOCKS_PER_STEP = 4
OUT_TM = 512
MLP_TM = 1024
MLP_TH = 1024
MLP_LAST_CHUNKS = 2
MLP_VMEM_LIMIT = 60 * 1024 * 1024

bf16 = jnp.bfloat16
f32 = jnp.float32


def _rms(x, g):
    return x * lax.rsqrt(jnp.mean(x * x, axis=-1, keepdims=True) + EPS) * g


def _dot(a, b):
    return jnp.dot(a, b, preferred_element_type=f32)


def _dot_nt(a, b):
    return lax.dot_general(a, b, (((1,), (1,)), ((), ())), preferred_element_type=f32)


def _rope128(t, cos, sin):
    lane = lax.broadcasted_iota(jnp.int32, t.shape, 1)
    fwd = pltpu.roll(t, QK_ROPE // 2, axis=1)
    bwd = pltpu.roll(t, LANES - QK_ROPE // 2, axis=1)
    rot = jnp.where((lane & (QK_ROPE // 2)) == 0, -bwd, fwd)
    return t * cos + rot * sin


_SRC_KR = Q_LORA + KV_LORA
_SRC_QS = _SRC_KR + QK_ROPE
_SRC_KS = _SRC_QS + SWA_WIDTH
_SRC_END = _SRC_KS + 2 * SWA_KV_WIDTH


def _win_kernel(w_ref, wq_ref, wkv_ref, o_ref, wq_o_ref, wkv_o_ref):
    wkv_o_ref[...] = wkv_ref[...].astype(bf16)
    hd = QK_NOPE + QK_ROPE
    for h in range(MLA_HEADS):
        wq_o_ref[:, h * QK_PAD:h * QK_PAD + hd] = wq_ref[:, h * hd:(h + 1) * hd].astype(bf16)
        wq_o_ref[:, h * QK_PAD + hd:(h + 1) * QK_PAD] = jnp.zeros(
            (wq_o_ref.shape[0], QK_PAD - hd), bf16)
    o_ref[_C_Q:_C_QS, :] = w_ref[0:_SRC_KR, :].astype(bf16)
    o_ref[_C_QS:_C_KS, :] = w_ref[_SRC_QS:_SRC_KS, :].astype(bf16)
    o_ref[_C_KS:_C_KR, :] = w_ref[_SRC_KS:_SRC_END, :].astype(bf16)
    o_ref[_C_KR:_C_KR + QK_ROPE, :] = w_ref[_SRC_KR:_SRC_QS, :].astype(bf16)
    o_ref[_C_KR + QK_ROPE:IN_PAD, :] = jnp.zeros(
        (IN_PAD - _C_KR - QK_ROPE, o_ref.shape[1]), bf16)


def _proj_kernel(x_ref, pos_ref, inv_ref, gin_ref, win_ref, gq_ref, wq_ref, gkv_ref, wkv_ref,
                 wd_ref, q_ref, k_ref, v_ref, qs_ref, ks_ref, vst_ref, wd_o_ref):
    wd_o_ref[...] = wd_ref[...].astype(bf16)
    a = _rms(x_ref[...], gin_ref[...])
    proj = _dot_nt(a.astype(bf16), win_ref[...])

    ang = pos_ref[...].astype(f32) * inv_ref[...]
    cos = jnp.cos(ang)
    sin = jnp.sin(ang)

    tm = proj.shape[0]
    qs_ref[...] = (proj[:, _C_QS:_C_QS + SWA_WIDTH] * SWA_SCALE).astype(bf16)
    ks_ref[...] = proj[:, _C_KS:_C_KS + LANES].astype(bf16)
    vt = proj[:, _C_VS:_C_VS + LANES].T.astype(bf16)
    ones = jnp.ones((SWA_HEAD_DIM, tm), bf16)
    for kvh in range(SWA_KV_HEADS):
        v_t = vt[kvh * SWA_HEAD_DIM:(kvh + 1) * SWA_HEAD_DIM]
        vst_ref[kvh * LANES:(kvh + 1) * LANES, :] = jnp.concatenate([v_t, ones], axis=0)

    cq = _rms(proj[:, _C_Q:_C_Q + Q_LORA], gq_ref[...])
    qf = _dot(cq.astype(bf16), wq_ref[...])
    ckv = _rms(proj[:, _C_KV:_C_KV + KV_LORA], gkv_ref[...])
    kvf = _dot(ckv.astype(bf16), wkv_ref[...])
    kr = _rope128(proj[:, _C_KR:_C_KR + LANES], cos, sin).astype(bf16)
    for h in range(MLA_HEADS):
        c0 = h * QK_PAD
        q_ref[:, c0:c0 + LANES] = (qf[:, c0:c0 + LANES] * MLA_SCALE).astype(bf16)
        q_ref[:, c0 + LANES:c0 + QK_PAD] = (
            _rope128(qf[:, c0 + LANES:c0 + QK_PAD], cos, sin) * MLA_SCALE).astype(bf16)
        k_ref[:, c0:c0 + LANES] = kvf[:, c0:c0 + LANES].astype(bf16)
        k_ref[:, c0 + LANES:c0 + QK_PAD] = kr
        v_ref[:, h * V_HEAD:(h + 1) * V_HEAD] = kvf[:, c0 + LANES:c0 + QK_PAD].astype(bf16)


def _mla_steps(q_ref, k_ref, v_ref, wu_ref, o_ref, wu_o_ref, s_ref, vt_ref):
    seq = q_ref.shape[0]
    tq = MLA_TQ
    n_tiles = seq // tq
    heads = range(MLA_HEADS_PER_STEP)
    order = list(reversed(range(n_tiles)))

    def base(i):
        return tq * (i * (i + 1) // 2)

    def scores(g, i):
        l = (i + 1) * tq
        s_ref[g, base(i):base(i) + l, :] = _dot_nt(
            k_ref[0:l, g * QK_PAD:(g + 1) * QK_PAD],
            q_ref[i * tq:(i + 1) * tq, g * QK_PAD:(g + 1) * QK_PAD])

    for g in heads:
        vt_ref[g, 0:V_HEAD, :] = v_ref[:, g * V_HEAD:(g + 1) * V_HEAD].T
        vt_ref[g, V_HEAD:, :] = jnp.ones((VT_ROWS - V_HEAD, seq), bf16)
        for ahead in range(MLA_STAGES - 1):
            scores(g, order[ahead])
    piece = wu_ref.shape[0] // n_tiles
    for step, i in enumerate(order):
        rows = pl.ds(step * piece, piece)
        wu_o_ref[rows, :] = wu_ref[rows, :].astype(bf16)
        for g in heads:
            if step + MLA_STAGES - 1 < n_tiles:
                scores(g, order[step + MLA_STAGES - 1])
            l0 = i * tq
            key = lax.broadcasted_iota(jnp.int32, (tq, tq), 0)
            qry = lax.broadcasted_iota(jnp.int32, (tq, tq), 1)
            sd = jnp.where(key <= qry, s_ref[g, base(i) + l0:base(i) + l0 + tq, :], -jnp.inf)
            m = jnp.max(sd, axis=0, keepdims=True)
            if i > 0:
                sa = s_ref[g, base(i):base(i) + l0, :]
                m = jnp.maximum(m, jnp.max(sa, axis=0, keepdims=True))
            acc = _dot(vt_ref[g, :, l0:l0 + tq], jnp.exp2(sd - m).astype(bf16))
            if i > 0:
                acc = acc + _dot(vt_ref[g, :, 0:l0], jnp.exp2(sa - m).astype(bf16))
            o_ref[l0:l0 + tq, g * V_HEAD:(g + 1) * V_HEAD] = (
                acc[0:V_HEAD] / acc[V_HEAD:V_HEAD + 1]).T.astype(o_ref.dtype)
        yield


def _swa_bias_table(rel_ref, tab_ref):
    kj = lax.broadcasted_iota(jnp.int32, (2 * BLOCK, BLOCK), 0)
    qi = lax.broadcasted_iota(jnp.int32, (2 * BLOCK, BLOCK), 1) + BLOCK
    dist = qi - kj
    in_window = (dist >= 0) & (dist < WINDOW)
    for h in range(SWA_Q_HEADS):
        acc = jnp.full((2 * BLOCK, BLOCK), rel_ref[0, h] * LOG2E, f32)
        for b in range(1, REL_BUCKETS):
            if T5_BUCKET_START[b] < WINDOW:
                acc = jnp.where(dist >= T5_BUCKET_START[b], rel_ref[b, h] * LOG2E, acc)
        tab_ref[0, :, h * BLOCK:(h + 1) * BLOCK] = jnp.where(in_window, acc, -jnp.inf)
        tab_ref[1, :, h * BLOCK:(h + 1) * BLOCK] = jnp.where(
            in_window & (kj >= BLOCK), acc, -jnp.inf)


def _swa_steps(sink_ref, q_ref, kp_ref, kc_ref, vp_ref, vc_ref, wo_ref, o_ref, wo_o_ref, tab_ref):
    wo_o_ref[...] = wo_ref[...].astype(bf16)
    hd = SWA_HEAD_DIM
    width = SWA_GROUP * BLOCK
    zeros = jnp.zeros((hd, BLOCK), bf16)
    sink = jnp.concatenate(
        [jnp.full((1, BLOCK), sink_ref[0, h] * LOG2E, f32) for h in range(SWA_Q_HEADS)], axis=1)
    k_all = jnp.concatenate([kp_ref[...], kc_ref[...]], axis=0)
    v_all = jnp.concatenate([vp_ref[...], vc_ref[...]], axis=1)
    def scores(j):
        rows = slice(j * BLOCK, (j + 1) * BLOCK)
        kband = k_all[j * BLOCK:(j + 2) * BLOCK]
        if j == 0:
            tab = tab_ref[(pl.program_id(1) == 0).astype(jnp.int32)]
        else:
            tab = tab_ref[0]
        q_t = q_ref[rows, :].T
        cols = []
        for h in range(SWA_Q_HEADS):
            qh = q_t[h * hd:(h + 1) * hd]
            cols.append(jnp.concatenate([qh, zeros] if h < SWA_GROUP else [zeros, qh], axis=0))
        return _dot(kband, jnp.concatenate(cols, axis=1)) + tab

    s_all = [scores(j) for j in range(SWA_BLOCKS_PER_STEP)]
    yield
    for j in range(SWA_BLOCKS_PER_STEP):
        rows = slice(j * BLOCK, (j + 1) * BLOCK)
        vband = v_all[:, j * BLOCK:(j + 2) * BLOCK]
        s = s_all[j]
        m = jnp.maximum(jnp.max(s, axis=0, keepdims=True), sink)
        p = jnp.exp2(s - m).astype(bf16)
        sink_term = jnp.exp2(sink - m)
        outs = []
        for c in range(SWA_KV_HEADS):
            r = _dot(vband[c * LANES:(c + 1) * LANES, :], p[:, c * width:(c + 1) * width])
            outs.append(r[0:hd] / (r[hd:hd + 1] + sink_term[:, c * width:(c + 1) * width]))
        for pair in range(SWA_Q_HEADS // 2):
            halves = []
            for h in (2 * pair, 2 * pair + 1):
                g = h % SWA_GROUP
                halves.append(outs[h // SWA_GROUP][:, g * BLOCK:(g + 1) * BLOCK])
            o_ref[rows, pair * LANES:(pair + 1) * LANES] = (
                jnp.concatenate(halves, axis=0).T.astype(o_ref.dtype))
        yield


def _attn_kernel(q_ref, k_ref, v_ref, wu_ref, rel_ref, sink_ref, qs_ref, kp_ref, kc_ref, vp_ref,
                 vc_ref, wo_ref, o_ref, wu_o_ref, os_ref, wo_o_ref, s_ref, vt_ref, tab_ref):
    @pl.when((pl.program_id(0) == 0) & (pl.program_id(1) == 0))
    def _():
        _swa_bias_table(rel_ref, tab_ref)

    mla = _mla_steps(q_ref, k_ref, v_ref, wu_ref, o_ref, wu_o_ref, s_ref, vt_ref)
    swa = _swa_steps(sink_ref, qs_ref, kp_ref, kc_ref, vp_ref, vc_ref, wo_ref,
                     os_ref, wo_o_ref, tab_ref)
    n_mla = q_ref.shape[0] // MLA_TQ
    every = n_mla // SWA_BLOCKS_PER_STEP
    next(swa)
    for t in range(n_mla):
        next(mla)
        if t % every == every - 1:
            next(swa)


def _out_kernel(x_ref, ya_ref, yb_ref, wa_ref, wb_ref, g_ref, h_ref, m_ref):
    h = x_ref[...] + _dot(ya_ref[...], wa_ref[...]) + _dot(yb_ref[...], wb_ref[...])
    h_ref[...] = h
    m_ref[...] = _rms(h, g_ref[...]).astype(m_ref.dtype)


def _mlp_kernel(m_ref, h_hbm, wu_ref, wd_ref, g_ref, o_ref, hbuf, sem):
    i = pl.program_id(0)
    j = pl.program_id(1)
    last_j = pl.num_programs(1) - 1
    tm = o_ref.shape[0]
    h_copy = pltpu.make_async_copy(h_hbm.at[pl.ds(i * tm, tm), :], hbuf, sem)

    def step(first, last):
        n_chunks = MLP_LAST_CHUNKS if last else 1
        rows = tm // n_chunks
        for c in range(n_chunks):
            r = pl.ds(c * rows, rows)
            u = jnp.maximum(_dot(m_ref[r, :], wu_ref[...]), 0.0)
            acc = _dot((u * u).astype(bf16), wd_ref[...])
            if not first:
                acc = o_ref[r, :] + acc
            if last:
                acc = _rms(acc + hbuf[r, :], g_ref[...])
            o_ref[r, :] = acc

    @pl.when(j == 0)
    def _():
        h_copy.start()
        step(first=True, last=False)

    @pl.when((j > 0) & (j < last_j))
    def _():
        step(first=False, last=False)

    @pl.when(j == last_j)
    def _():
        h_copy.wait()
        step(first=False, last=True)


def _params(sem):
    return pltpu.CompilerParams(dimension_semantics=sem, vmem_limit_bytes=VMEM_LIMIT)


def kernel(x, positions, rel_bias, attn_norm, w_in, q_a_norm, w_q_b, kv_a_norm, w_kv_b, sinks,
           w_out, mlp_norm, w_up, w_down, final_norm):
    batch, seq, d = x.shape
    tokens = batch * seq
    nb = seq // BLOCK
    depth = w_in.shape[0]
    assert depth == 1 and d == D_MODEL and seq % MLA_TQ == 0

    assert w_in.shape[2] == _SRC_END
    n_prep = d // WIN_ROWS
    lat_rows = Q_LORA // n_prep
    assert KV_LORA == Q_LORA
    slab3 = lambda rows, cols: pl.BlockSpec((None, rows, cols), lambda i: (0, i, 0))
    slab2 = lambda rows, cols: pl.BlockSpec((rows, cols), lambda i: (i, 0))
    kv_cols = MLA_HEADS * (QK_NOPE + V_HEAD)
    win, wq, wkv = pl.pallas_call(
        _win_kernel,
        grid=(n_prep,),
        in_specs=[pl.BlockSpec((_SRC_END, WIN_ROWS), lambda i: (0, i)),
                  slab3(lat_rows, MLA_HEADS * (QK_NOPE + QK_ROPE)), slab3(lat_rows, kv_cols)],
        out_specs=[pl.BlockSpec((IN_PAD, WIN_ROWS), lambda i: (0, i)),
                   slab2(lat_rows, MLA_HEADS * QK_PAD), slab2(lat_rows, kv_cols)],
        out_shape=[jax.ShapeDtypeStruct((IN_PAD, d), bf16),
                   jax.ShapeDtypeStruct((Q_LORA, MLA_HEADS * QK_PAD), bf16),
                   jax.ShapeDtypeStruct((KV_LORA, kv_cols), bf16)],
        compiler_params=_params(("arbitrary",)),
        name="winprep",
    )(jnp.transpose(w_in[0]), w_q_b, w_kv_b)

    inv = 1.0 / (ROPE_THETA ** (jnp.arange(0, QK_ROPE, 2, dtype=f32) / QK_ROPE))
    inv128 = jnp.concatenate([inv, inv, jnp.zeros((LANES - QK_ROPE,), f32)])[None, :]

    x2 = x.reshape(tokens, d)
    pos2 = positions.reshape(tokens, 1)
    row = lambda v: v.reshape(1, -1)

    tm = PROJ_TM
    full = lambda shape: pl.BlockSpec(shape, lambda i: (0, 0))
    tok = lambda w: pl.BlockSpec((tm, w), lambda i: (i, 0))
    vt_rows = SWA_KV_HEADS * LANES
    wd_slab = pl.BlockSpec((MLP_HIDDEN // (tokens // tm), d), lambda i: (i, 0))
    q, k, v, qs, ks, vst, wd = pl.pallas_call(
        _proj_kernel,
        grid=(tokens // tm,),
        in_specs=[tok(d), tok(1), full((1, LANES)), full((1, d)), full((IN_PAD, d)),
                  full((1, Q_LORA)), full((Q_LORA, MLA_HEADS * QK_PAD)),
                  full((1, KV_LORA)), full((KV_LORA, MLA_HEADS * (QK_NOPE + V_HEAD))),
                  wd_slab],
        out_specs=[tok(MLA_HEADS * QK_PAD), tok(MLA_HEADS * QK_PAD), tok(MLA_WIDTH),
                   tok(SWA_WIDTH), tok(SWA_KV_WIDTH),
                   pl.BlockSpec((vt_rows, tm), lambda i: (0, i)), wd_slab],
        out_shape=[jax.ShapeDtypeStruct((tokens, MLA_HEADS * QK_PAD), bf16),
                   jax.ShapeDtypeStruct((tokens, MLA_HEADS * QK_PAD), bf16),
                   jax.ShapeDtypeStruct((tokens, MLA_WIDTH), bf16),
                   jax.ShapeDtypeStruct((tokens, SWA_WIDTH), bf16),
                   jax.ShapeDtypeStruct((tokens, SWA_KV_WIDTH), bf16),
                   jax.ShapeDtypeStruct((vt_rows, tokens), bf16),
                   jax.ShapeDtypeStruct((MLP_HIDDEN, d), bf16)],
        compiler_params=_params(("arbitrary",)),
        name="proj",
    )(x2, pos2, inv128, row(attn_norm[0]), win, row(q_a_norm[0]), wq, row(kv_a_norm[0]), wkv,
      w_down[0])

    hps = MLA_HEADS_PER_STEP
    bps = SWA_BLOCKS_PER_STEP
    steps = MLA_HEADS // hps
    assert nb // bps == steps
    rows = bps * BLOCK
    smem = pl.BlockSpec(memory_space=pltpu.SMEM)
    wu_slab = pl.BlockSpec((d // (batch * steps), MLP_HIDDEN), lambda b, t: (b * steps + t, 0))
    wo_slab = pl.BlockSpec(((MLA_WIDTH + SWA_WIDTH) // (batch * steps), d),
                           lambda b, t: (b * steps + t, 0))
    prev = lambda b, t: b * nb + jnp.maximum(t * bps - 1, 0)
    y_mla, wu, y_swa, wo = pl.pallas_call(
        _attn_kernel,
        grid=(batch, steps),
        in_specs=[pl.BlockSpec((seq, hps * QK_PAD), lambda b, t: (b, t)),
                  pl.BlockSpec((seq, hps * QK_PAD), lambda b, t: (b, t)),
                  pl.BlockSpec((seq, hps * V_HEAD), lambda b, t: (b, t)),
                  wu_slab,
                  smem, smem,
                  pl.BlockSpec((rows, SWA_WIDTH), lambda b, t: (b * steps + t, 0)),
                  pl.BlockSpec((BLOCK, SWA_KV_WIDTH), lambda b, t: (prev(b, t), 0)),
                  pl.BlockSpec((rows, SWA_KV_WIDTH), lambda b, t: (b * steps + t, 0)),
                  pl.BlockSpec((vt_rows, BLOCK), lambda b, t: (0, prev(b, t))),
                  pl.BlockSpec((vt_rows, rows), lambda b, t: (0, b * steps + t)),
                  wo_slab],
        out_specs=[pl.BlockSpec((seq, hps * V_HEAD), lambda b, t: (b, t)), wu_slab,
                   pl.BlockSpec((rows, SWA_WIDTH), lambda b, t: (b * steps + t, 0)), wo_slab],
        out_shape=[jax.ShapeDtypeStruct((tokens, MLA_WIDTH), bf16),
                   jax.ShapeDtypeStruct((d, MLP_HIDDEN), bf16),
                   jax.ShapeDtypeStruct((tokens, SWA_WIDTH), bf16),
                   jax.ShapeDtypeStruct((MLA_WIDTH + SWA_WIDTH, d), bf16)],
        scratch_shapes=[pltpu.VMEM((hps, (seq // MLA_TQ) * (seq // MLA_TQ + 1) // 2 * MLA_TQ,
                                    MLA_TQ), f32),
                        pltpu.VMEM((hps, VT_ROWS, seq), bf16),
                        pltpu.VMEM((2, 2 * BLOCK, SWA_Q_HEADS * BLOCK), f32)],
        compiler_params=_params(("arbitrary", "arbitrary")),
        name="attn",
    )(q, k, v, w_up[0], rel_bias, row(sinks[0]), qs, ks, ks, vst, vst, w_out[0])

    tm = OUT_TM
    h1, m = pl.pallas_call(
        _out_kernel,
        grid=(tokens // tm,),
        in_specs=[tok(d), tok(MLA_WIDTH), tok(SWA_WIDTH),
                  pl.BlockSpec((MLA_WIDTH, d), lambda i: (0, 0)),
                  pl.BlockSpec((SWA_WIDTH, d), lambda i: (1, 0)), full((1, d))],
        out_specs=[tok(d), tok(d)],
        out_shape=[jax.ShapeDtypeStruct((tokens, d), f32),
                   jax.ShapeDtypeStruct((tokens, d), bf16)],
        compiler_params=_params(("arbitrary",)),
        name="outproj",
    )(x2, y_mla, y_swa, wo, wo, row(mlp_norm[0]))

    tm, th = MLP_TM, MLP_TH
    assert MLP_HIDDEN // th >= 2
    out = pl.pallas_call(
        _mlp_kernel,
        grid=(tokens // tm, MLP_HIDDEN // th),
        in_specs=[pl.BlockSpec((tm, d), lambda i, j: (i, 0)),
                  pl.BlockSpec(memory_space=pl.ANY),
                  pl.BlockSpec((d, th), lambda i, j: (0, j)),
                  pl.BlockSpec((th, d), lambda i, j: (j, 0)),
                  pl.BlockSpec((1, d), lambda i, j: (0, 0))],
        out_specs=pl.BlockSpec((tm, d), lambda i, j: (i, 0)),
        out_shape=jax.ShapeDtypeStruct((tokens, d), f32),
        scratch_shapes=[pltpu.VMEM((tm, d), f32), pltpu.SemaphoreType.DMA(())],
        compiler_params=pltpu.CompilerParams(
            dimension_semantics=("arbitrary", "arbitrary"), vmem_limit_bytes=MLP_VMEM_LIMIT),
        name="mlp",
    )(m, h1, wu, wd, row(final_norm))

    return out.reshape(batch, seq, d)
```

```python
import jax
import jax.numpy as jnp
import numpy as np
from jax import lax
from jax.experimental import pallas as pl
from jax.experimental.pallas import tpu as pltpu

D_MODEL = 2048
MLA_HEADS = 8
QK_NOPE = 128
QK_ROPE = 64
V_HEAD = 128
Q_LORA = 512
KV_LORA = 512
ROPE_THETA = 10000.0
SWA_Q_HEADS = 16
SWA_KV_HEADS = 2
SWA_GROUP = SWA_Q_HEADS // SWA_KV_HEADS
SWA_HEAD_DIM = 64
WINDOW = 128
BLOCK = 128
REL_BUCKETS = 32
REL_MAX_DIST = 128
MLP_HIDDEN = 4 * D_MODEL
EPS = 1e-6
MLA_WIDTH = MLA_HEADS * V_HEAD
SWA_WIDTH = SWA_Q_HEADS * SWA_HEAD_DIM
SWA_KV_WIDTH = SWA_KV_HEADS * SWA_HEAD_DIM

LANES = 128
QK_PAD = 2 * LANES
BF16_SUBLANES = 16
VT_ROWS = V_HEAD + BF16_SUBLANES
LOG2E = float(np.log2(np.e))
MLA_SCALE = (QK_NOPE + QK_ROPE) ** -0.5 * LOG2E
SWA_SCALE = SWA_HEAD_DIM ** -0.5 * LOG2E

_C_Q = 0
_C_KV = _C_Q + Q_LORA
_C_QS = _C_KV + KV_LORA
_C_KS = _C_QS + SWA_WIDTH
_C_VS = _C_KS + SWA_KV_WIDTH
_C_KR = _C_VS + SWA_KV_WIDTH
IN_PAD = _C_KR + LANES

def _t5_bucket_starts():
    max_exact = REL_BUCKETS // 2
    dist = np.arange(WINDOW)
    val = (np.log(np.maximum(dist, 1).astype(np.float32) / np.float32(max_exact))
           / np.float32(np.log(REL_MAX_DIST / max_exact)) * np.float32(REL_BUCKETS - max_exact))
    margin = np.abs(val - np.round(val))[max_exact + 1:]
    assert margin.min() > 1e-3, margin.min()
    large = np.minimum(max_exact + np.floor(val).astype(np.int64), REL_BUCKETS - 1)
    bucket = np.where(dist < max_exact, dist, large)
    assert np.all(np.diff(bucket) >= 0)
    return tuple(int(np.argmax(bucket >= b)) if np.any(bucket >= b) else WINDOW
                 for b in range(REL_BUCKETS))


T5_BUCKET_START = _t5_bucket_starts()

VMEM_LIMIT = 56 * 1024 * 1024

WIN_ROWS = 512
PROJ_TM = 512
MLA_TQ = 256
MLA_HEADS_PER_STEP = 2
MLA_STAGES = 4
SWA_BLOCKS_PER_STEP = 4
OUT_TM = 512
MLP_TM = 1024
MLP_TH = 1024
MLP_LAST_CHUNKS = 2
MLP_VMEM_LIMIT = 60 * 1024 * 1024

bf16 = jnp.bfloat16
f32 = jnp.float32


def _rms(x, g):
    return x * lax.rsqrt(jnp.mean(x * x, axis=-1, keepdims=True) + EPS) * g


def _dot(a, b):
    return jnp.dot(a, b, preferred_element_type=f32)


def _dot_nt(a, b):
    return lax.dot_general(a, b, (((1,), (1,)), ((), ())), preferred_element_type=f32)


def _rope128(t, cos, sin):
    lane = lax.broadcasted_iota(jnp.int32, t.shape, 1)
    fwd = pltpu.roll(t, QK_ROPE // 2, axis=1)
    bwd = pltpu.roll(t, LANES - QK_ROPE // 2, axis=1)
    rot = jnp.where((lane & (QK_ROPE // 2)) == 0, -bwd, fwd)
    return t * cos + rot * sin


_SRC_KR = Q_LORA + KV_LORA
_SRC_QS = _SRC_KR + QK_ROPE
_SRC_KS = _SRC_QS + SWA_WIDTH
_SRC_END = _SRC_KS + 2 * SWA_KV_WIDTH


def _win_kernel(w_ref, wq_ref, wkv_ref, o_ref, wq_o_ref, wkv_o_ref):
    wkv_o_ref[...] = wkv_ref[...].astype(bf16)
    hd = QK_NOPE + QK_ROPE
    for h in range(MLA_HEADS):
        wq_o_ref[:, h * QK_NOPE:(h + 1) * QK_NOPE] = (
            wq_ref[:, h * hd:h * hd + QK_NOPE].astype(bf16))
        r0 = MLA_HEADS * QK_NOPE + h * QK_ROPE
        wq_o_ref[:, r0:r0 + QK_ROPE] = wq_ref[:, h * hd + QK_NOPE:(h + 1) * hd].astype(bf16)
    o_ref[_C_Q:_C_QS, :] = w_ref[0:_SRC_KR, :].astype(bf16)
    o_ref[_C_QS:_C_KS, :] = w_ref[_SRC_QS:_SRC_KS, :].astype(bf16)
    o_ref[_C_KS:_C_KR, :] = w_ref[_SRC_KS:_SRC_END, :].astype(bf16)
    o_ref[_C_KR:_C_KR + QK_ROPE, :] = w_ref[_SRC_KR:_SRC_QS, :].astype(bf16)
    o_ref[_C_KR + QK_ROPE:IN_PAD, :] = jnp.zeros(
        (IN_PAD - _C_KR - QK_ROPE, o_ref.shape[1]), bf16)


def _proj_kernel(x_ref, pos_ref, inv_ref, gin_ref, win_ref, gq_ref, wq_ref, gkv_ref, wkv_ref,
                 wd_ref, q_ref, k_ref, v_ref, qs_ref, ks_ref, vst_ref, wd_o_ref):
    wd_o_ref[...] = wd_ref[...].astype(bf16)
    a = _rms(x_ref[...], gin_ref[...])
    proj = _dot_nt(a.astype(bf16), win_ref[...])

    ang = pos_ref[...].astype(f32) * inv_ref[...]
    cos = jnp.cos(ang)
    sin = jnp.sin(ang)

    tm = proj.shape[0]
    qs_ref[...] = (proj[:, _C_QS:_C_QS + SWA_WIDTH] * SWA_SCALE).astype(bf16)
    ks_ref[...] = proj[:, _C_KS:_C_KS + LANES].astype(bf16)
    vt = proj[:, _C_VS:_C_VS + LANES].T.astype(bf16)
    ones = jnp.ones((SWA_HEAD_DIM, tm), bf16)
    for kvh in range(SWA_KV_HEADS):
        v_t = vt[kvh * SWA_HEAD_DIM:(kvh + 1) * SWA_HEAD_DIM]
        vst_ref[kvh * LANES:(kvh + 1) * LANES, :] = jnp.concatenate([v_t, ones], axis=0)

    cq = _rms(proj[:, _C_Q:_C_Q + Q_LORA], gq_ref[...])
    qf = _dot(cq.astype(bf16), wq_ref[...])
    ckv = _rms(proj[:, _C_KV:_C_KV + KV_LORA], gkv_ref[...])
    kvf = _dot(ckv.astype(bf16), wkv_ref[...])
    kr = _rope128(proj[:, _C_KR:_C_KR + LANES], cos, sin).astype(bf16)
    rope0 = MLA_HEADS * QK_NOPE
    for h in range(MLA_HEADS):
        c0 = h * QK_PAD
        q_ref[:, c0:c0 + LANES] = (qf[:, h * QK_NOPE:(h + 1) * QK_NOPE] * MLA_SCALE).astype(bf16)
        if h % 2 == 0:
            pair = _rope128(qf[:, rope0 + h * QK_ROPE:rope0 + h * QK_ROPE + LANES], cos, sin)
            pair = pair * MLA_SCALE
            own = pair
        else:
            own = pltpu.roll(pair, QK_ROPE, axis=1)
        q_ref[:, c0 + LANES:c0 + QK_PAD] = own.astype(bf16)
        k_ref[:, c0:c0 + LANES] = kvf[:, c0:c0 + LANES].astype(bf16)
        k_ref[:, c0 + LANES:c0 + QK_PAD] = kr
        v_ref[:, h * V_HEAD:(h + 1) * V_HEAD] = kvf[:, c0 + LANES:c0 + QK_PAD].astype(bf16)


def _mla_steps(q_ref, k_ref, v_ref, wu_ref, o_ref, wu_o_ref, s_ref, vt_ref):
    seq = q_ref.shape[0]
    tq = MLA_TQ
    n_tiles = seq // tq
    heads = range(MLA_HEADS_PER_STEP)
    order = list(reversed(range(n_tiles)))

    def base(i):
        return tq * (i * (i + 1) // 2)

    def scores(g, i):
        l = (i + 1) * tq
        s_ref[g, base(i):base(i) + l, :] = _dot_nt(
            k_ref[0:l, g * QK_PAD:(g + 1) * QK_PAD],
            q_ref[i * tq:(i + 1) * tq, g * QK_PAD:(g + 1) * QK_PAD])

    for g in heads:
        vt_ref[g, 0:V_HEAD, :] = v_ref[:, g * V_HEAD:(g + 1) * V_HEAD].T
        vt_ref[g, V_HEAD:, :] = jnp.ones((VT_ROWS - V_HEAD, seq), bf16)
        for ahead in range(MLA_STAGES - 1):
            scores(g, order[ahead])
    piece = wu_ref.shape[0] // n_tiles
    for step, i in enumerate(order):
        rows = pl.ds(step * piece, piece)
        wu_o_ref[rows, :] = wu_ref[rows, :].astype(bf16)
        for g in heads:
            if step + MLA_STAGES - 1 < n_tiles:
                scores(g, order[step + MLA_STAGES - 1])
            l0 = i * tq
            key = lax.broadcasted_iota(jnp.int32, (tq, tq), 0)
            qry = lax.broadcasted_iota(jnp.int32, (tq, tq), 1)
            sd = jnp.where(key <= qry, s_ref[g, base(i) + l0:base(i) + l0 + tq, :], -jnp.inf)
            m = jnp.max(sd, axis=0, keepdims=True)
            if i > 0:
                sa = s_ref[g, base(i):base(i) + l0, :]
                m = jnp.maximum(m, jnp.max(sa, axis=0, keepdims=True))
            acc = _dot(vt_ref[g, :, l0:l0 + tq], jnp.exp2(sd - m).astype(bf16))
            if i > 0:
                acc = acc + _dot(vt_ref[g, :, 0:l0], jnp.exp2(sa - m).astype(bf16))
            o_ref[l0:l0 + tq, g * V_HEAD:(g + 1) * V_HEAD] = (
                acc[0:V_HEAD] / acc[V_HEAD:V_HEAD + 1]).T.astype(o_ref.dtype)
        yield


def _swa_bias_table(rel_ref, tab_ref):
    kj = lax.broadcasted_iota(jnp.int32, (2 * BLOCK, BLOCK), 0)
    qi = lax.broadcasted_iota(jnp.int32, (2 * BLOCK, BLOCK), 1) + BLOCK
    dist = qi - kj
    in_window = (dist >= 0) & (dist < WINDOW)
    for h in range(SWA_Q_HEADS):
        acc = jnp.full((2 * BLOCK, BLOCK), rel_ref[0, h] * LOG2E, f32)
        for b in range(1, REL_BUCKETS):
            if T5_BUCKET_START[b] < WINDOW:
                acc = jnp.where(dist >= T5_BUCKET_START[b], rel_ref[b, h] * LOG2E, acc)
        tab_ref[0, :, h * BLOCK:(h + 1) * BLOCK] = jnp.where(in_window, acc, -jnp.inf)
        tab_ref[1, :, h * BLOCK:(h + 1) * BLOCK] = jnp.where(
            in_window & (kj >= BLOCK), acc, -jnp.inf)


def _swa_steps(sink_ref, q_ref, kp_ref, kc_ref, vp_ref, vc_ref, wo_ref, o_ref, wo_o_ref, tab_ref):
    wo_o_ref[...] = wo_ref[...].astype(bf16)
    hd = SWA_HEAD_DIM
    width = SWA_GROUP * BLOCK
    zeros = jnp.zeros((hd, BLOCK), bf16)
    sink = jnp.concatenate(
        [jnp.full((1, BLOCK), sink_ref[0, h] * LOG2E, f32) for h in range(SWA_Q_HEADS)], axis=1)
    k_all = jnp.concatenate([kp_ref[...], kc_ref[...]], axis=0)
    v_all = jnp.concatenate([vp_ref[...], vc_ref[...]], axis=1)
    def scores(j):
        rows = slice(j * BLOCK, (j + 1) * BLOCK)
        kband = k_all[j * BLOCK:(j + 2) * BLOCK]
        if j == 0:
            tab = tab_ref[(pl.program_id(1) == 0).astype(jnp.int32)]
        else:
            tab = tab_ref[0]
        q_t = q_ref[rows, :].T
        cols = []
        for h in range(SWA_Q_HEADS):
            qh = q_t[h * hd:(h + 1) * hd]
            cols.append(jnp.concatenate([qh, zeros] if h < SWA_GROUP else [zeros, qh], axis=0))
        return _dot(kband, jnp.concatenate(cols, axis=1)) + tab

    s_all = [scores(j) for j in range(SWA_BLOCKS_PER_STEP)]
    yield
    for j in range(SWA_BLOCKS_PER_STEP):
        rows = slice(j * BLOCK, (j + 1) * BLOCK)
        vband = v_all[:, j * BLOCK:(j + 2) * BLOCK]
        s = s_all[j]
        m = jnp.maximum(jnp.max(s, axis=0, keepdims=True), sink)
        p = jnp.exp2(s - m).astype(bf16)
        sink_term = jnp.exp2(sink - m)
        outs = []
        for c in range(SWA_KV_HEADS):
            r = _dot(vband[c * LANES:(c + 1) * LANES, :], p[:, c * width:(c + 1) * width])
            outs.append(r[0:hd] / (r[hd:hd + 1] + sink_term[:, c * width:(c + 1) * width]))
        for pair in range(SWA_Q_HEADS // 2):
            halves = []
            for h in (2 * pair, 2 * pair + 1):
                g = h % SWA_GROUP
                halves.append(outs[h // SWA_GROUP][:, g * BLOCK:(g + 1) * BLOCK])
            o_ref[rows, pair * LANES:(pair + 1) * LANES] = (
                jnp.concatenate(halves, axis=0).T.astype(o_ref.dtype))
        yield


def _attn_kernel(q_ref, k_ref, v_ref, wu_ref, rel_ref, sink_ref, qs_ref, kp_ref, kc_ref, vp_ref,
                 vc_ref, wo_ref, o_ref, wu_o_ref, os_ref, wo_o_ref, s_ref, vt_ref, tab_ref):
    @pl.when((pl.program_id(0) == 0) & (pl.program_id(1) == 0))
    def _():
        _swa_bias_table(rel_ref, tab_ref)

    mla = _mla_steps(q_ref, k_ref, v_ref, wu_ref, o_ref, wu_o_ref, s_ref, vt_ref)
    swa = _swa_steps(sink_ref, qs_ref, kp_ref, kc_ref, vp_ref, vc_ref, wo_ref,
                     os_ref, wo_o_ref, tab_ref)
    n_mla = q_ref.shape[0] // MLA_TQ
    every = n_mla // SWA_BLOCKS_PER_STEP
    next(swa)
    for t in range(n_mla):
        next(mla)
        if t % every == every - 1:
            next(swa)


def _out_kernel(x_ref, ya_ref, yb_ref, wa_ref, wb_ref, g_ref, h_ref, m_ref):
    h = x_ref[...] + _dot(ya_ref[...], wa_ref[...]) + _dot(yb_ref[...], wb_ref[...])
    h_ref[...] = h
    m_ref[...] = _rms(h, g_ref[...]).astype(m_ref.dtype)


def _mlp_kernel(m_ref, h_hbm, wu_ref, wd_ref, g_ref, o_ref, hbuf, sem):
    i = pl.program_id(0)
    j = pl.program_id(1)
    last_j = pl.num_programs(1) - 1
    tm = o_ref.shape[0]
    h_copy = pltpu.make_async_copy(h_hbm.at[pl.ds(i * tm, tm), :], hbuf, sem)

    def step(first, last):
        n_chunks = MLP_LAST_CHUNKS if last else 1
        rows = tm // n_chunks
        for c in range(n_chunks):
            r = pl.ds(c * rows, rows)
            u = jnp.maximum(_dot(m_ref[r, :], wu_ref[...]), 0.0)
            acc = _dot((u * u).astype(bf16), wd_ref[...])
            if not first:
                acc = o_ref[r, :] + acc
            if last:
                acc = _rms(acc + hbuf[r, :], g_ref[...])
            o_ref[r, :] = acc

    @pl.when(j == 0)
    def _():
        h_copy.start()
        step(first=True, last=False)

    @pl.when((j > 0) & (j < last_j))
    def _():
        step(first=False, last=False)

    @pl.when(j == last_j)
    def _():
        h_copy.wait()
        step(first=False, last=True)


def _params(sem):
    return pltpu.CompilerParams(dimension_semantics=sem, vmem_limit_bytes=VMEM_LIMIT)


def kernel(x, positions, rel_bias, attn_norm, w_in, q_a_norm, w_q_b, kv_a_norm, w_kv_b, sinks,
           w_out, mlp_norm, w_up, w_down, final_norm):
    batch, seq, d = x.shape
    tokens = batch * seq
    nb = seq // BLOCK
    depth = w_in.shape[0]
    assert depth == 1 and d == D_MODEL and seq % MLA_TQ == 0

    assert w_in.shape[2] == _SRC_END
    n_prep = d // WIN_ROWS
    lat_rows = Q_LORA // n_prep
    assert KV_LORA == Q_LORA
    slab3 = lambda rows, cols: pl.BlockSpec((None, rows, cols), lambda i: (0, i, 0))
    slab2 = lambda rows, cols: pl.BlockSpec((rows, cols), lambda i: (i, 0))
    kv_cols = MLA_HEADS * (QK_NOPE + V_HEAD)
    q_cols = MLA_HEADS * (QK_NOPE + QK_ROPE)
    win, wq, wkv = pl.pallas_call(
        _win_kernel,
        grid=(n_prep,),
        in_specs=[pl.BlockSpec((_SRC_END, WIN_ROWS), lambda i: (0, i)),
                  slab3(lat_rows, q_cols), slab3(lat_rows, kv_cols)],
        out_specs=[pl.BlockSpec((IN_PAD, WIN_ROWS), lambda i: (0, i)),
                   slab2(lat_rows, q_cols), slab2(lat_rows, kv_cols)],
        out_shape=[jax.ShapeDtypeStruct((IN_PAD, d), bf16),
                   jax.ShapeDtypeStruct((Q_LORA, q_cols), bf16),
                   jax.ShapeDtypeStruct((KV_LORA, kv_cols), bf16)],
        compiler_params=_params(("arbitrary",)),
        name="winprep",
    )(jnp.transpose(w_in[0]), w_q_b, w_kv_b)

    inv = 1.0 / (ROPE_THETA ** (jnp.arange(0, QK_ROPE, 2, dtype=f32) / QK_ROPE))
    inv128 = jnp.tile(inv, LANES // (QK_ROPE // 2))[None, :]

    x2 = x.reshape(tokens, d)
    pos2 = positions.reshape(tokens, 1)
    row = lambda v: v.reshape(1, -1)

    tm = PROJ_TM
    full = lambda shape: pl.BlockSpec(shape, lambda i: (0, 0))
    tok = lambda w: pl.BlockSpec((tm, w), lambda i: (i, 0))
    vt_rows = SWA_KV_HEADS * LANES
    wd_slab = pl.BlockSpec((MLP_HIDDEN // (tokens // tm), d), lambda i: (i, 0))
    q, k, v, qs, ks, vst, wd = pl.pallas_call(
        _proj_kernel,
        grid=(tokens // tm,),
        in_specs=[tok(d), tok(1), full((1, LANES)), full((1, d)), full((IN_PAD, d)),
                  full((1, Q_LORA)), full((Q_LORA, q_cols)),
                  full((1, KV_LORA)), full((KV_LORA, MLA_HEADS * (QK_NOPE + V_HEAD))),
                  wd_slab],
        out_specs=[tok(MLA_HEADS * QK_PAD), tok(MLA_HEADS * QK_PAD), tok(MLA_WIDTH),
                   tok(SWA_WIDTH), tok(SWA_KV_WIDTH),
                   pl.BlockSpec((vt_rows, tm), lambda i: (0, i)), wd_slab],
        out_shape=[jax.ShapeDtypeStruct((tokens, MLA_HEADS * QK_PAD), bf16),
                   jax.ShapeDtypeStruct((tokens, MLA_HEADS * QK_PAD), bf16),
                   jax.ShapeDtypeStruct((tokens, MLA_WIDTH), bf16),
                   jax.ShapeDtypeStruct((tokens, SWA_WIDTH), bf16),
                   jax.ShapeDtypeStruct((tokens, SWA_KV_WIDTH), bf16),
                   jax.ShapeDtypeStruct((vt_rows, tokens), bf16),
                   jax.ShapeDtypeStruct((MLP_HIDDEN, d), bf16)],
        compiler_params=_params(("arbitrary",)),
        name="proj",
    )(x2, pos2, inv128, row(attn_norm[0]), win, row(q_a_norm[0]), wq, row(kv_a_norm[0]), wkv,
      w_down[0])

    hps = MLA_HEADS_PER_STEP
    bps = SWA_BLOCKS_PER_STEP
    steps = MLA_HEADS // hps
    assert nb // bps == steps
    rows = bps * BLOCK
    smem = pl.BlockSpec(memory_space=pltpu.SMEM)
    wu_slab = pl.BlockSpec((d // (batch * steps), MLP_HIDDEN), lambda b, t: (b * steps + t, 0))
    wo_slab = pl.BlockSpec(((MLA_WIDTH + SWA_WIDTH) // (batch * steps), d),
                           lambda b, t: (b * steps + t, 0))
    prev = lambda b, t: b * nb + jnp.maximum(t * bps - 1, 0)
    y_mla, wu, y_swa, wo = pl.pallas_call(
        _attn_kernel,
        grid=(batch, steps),
        in_specs=[pl.BlockSpec((seq, hps * QK_PAD), lambda b, t: (b, t)),
                  pl.BlockSpec((seq, hps * QK_PAD), lambda b, t: (b, t)),
                  pl.BlockSpec((seq, hps * V_HEAD), lambda b, t: (b, t)),
                  wu_slab,
                  smem, smem,
                  pl.BlockSpec((rows, SWA_WIDTH), lambda b, t: (b * steps + t, 0)),
                  pl.BlockSpec((BLOCK, SWA_KV_WIDTH), lambda b, t: (prev(b, t), 0)),
                  pl.BlockSpec((rows, SWA_KV_WIDTH), lambda b, t: (b * steps + t, 0)),
                  pl.BlockSpec((vt_rows, BLOCK), lambda b, t: (0, prev(b, t))),
                  pl.BlockSpec((vt_rows, rows), lambda b, t: (0, b * steps + t)),
                  wo_slab],
        out_specs=[pl.BlockSpec((seq, hps * V_HEAD), lambda b, t: (b, t)), wu_slab,
                   pl.BlockSpec((rows, SWA_WIDTH), lambda b, t: (b * steps + t, 0)), wo_slab],
        out_shape=[jax.ShapeDtypeStruct((tokens, MLA_WIDTH), bf16),
                   jax.ShapeDtypeStruct((d, MLP_HIDDEN), bf16),
                   jax.ShapeDtypeStruct((tokens, SWA_WIDTH), bf16),
                   jax.ShapeDtypeStruct((MLA_WIDTH + SWA_WIDTH, d), bf16)],
        scratch_shapes=[pltpu.VMEM((hps, (seq // MLA_TQ) * (seq // MLA_TQ + 1) // 2 * MLA_TQ,
                                    MLA_TQ), f32),
                        pltpu.VMEM((hps, VT_ROWS, seq), bf16),
                        pltpu.VMEM((2, 2 * BLOCK, SWA_Q_HEADS * BLOCK), f32)],
        compiler_params=_params(("arbitrary", "arbitrary")),
        name="attn",
    )(q, k, v, w_up[0], rel_bias, row(sinks[0]), qs, ks, ks, vst, vst, w_out[0])

    tm = OUT_TM
    h1, m = pl.pallas_call(
        _out_kernel,
        grid=(tokens // tm,),
        in_specs=[tok(d), tok(MLA_WIDTH), tok(SWA_WIDTH),
                  pl.BlockSpec((MLA_WIDTH, d), lambda i: (0, 0)),
                  pl.BlockSpec((SWA_WIDTH, d), lambda i: (1, 0)), full((1, d))],
        out_specs=[tok(d), tok(d)],
        out_shape=[jax.ShapeDtypeStruct((tokens, d), f32),
                   jax.ShapeDtypeStruct((tokens, d), bf16)],
        compiler_params=_params(("arbitrary",)),
        name="outproj",
    )(x2, y_mla, y_swa, wo, wo, row(mlp_norm[0]))

    tm, th = MLP_TM, MLP_TH
    assert MLP_HIDDEN // th >= 2
    out = pl.pallas_call(
        _mlp_kernel,
        grid=(tokens // tm, MLP_HIDDEN // th),
        in_specs=[pl.BlockSpec((tm, d), lambda i, j: (i, 0)),
                  pl.BlockSpec(memory_space=pl.ANY),
                  pl.BlockSpec((d, th), lambda i, j: (0, j)),
                  pl.BlockSpec((th, d), lambda i, j: (j, 0)),
                  pl.BlockSpec((1, d), lambda i, j: (0, 0))],
        out_specs=pl.BlockSpec((tm, d), lambda i, j: (i, 0)),
        out_shape=jax.ShapeDtypeStruct((tokens, d), f32),
        scratch_shapes=[pltpu.VMEM((tm, d), f32), pltpu.SemaphoreType.DMA(())],
        compiler_params=pltpu.CompilerParams(
            dimension_semantics=("arbitrary", "arbitrary"), vmem_limit_bytes=MLP_VMEM_LIMIT),
        name="mlp",
    )(m, h1, wu, wd, row(final_norm))

    return out.reshape(batch, seq, d)
```
